```python
import math
import jax
import jax.numpy as jnp
from jax import lax
import numpy as np

D_MODEL = 2048
BATCH = 4
SEQ = 2048
DEPTH = 2

HEAD_DIM = 128
N_MIX_HEADS = D_MODEL // HEAD_DIM
NSA_HEADS = N_MIX_HEADS // 4
SWA_HEADS = N_MIX_HEADS // 4
GDN_HEADS = N_MIX_HEADS - NSA_HEADS - SWA_HEADS
SWA_KV_HEADS = max(1, SWA_HEADS // 2)
N_SOFTMAX_HEADS = NSA_HEADS + SWA_HEADS
NSA_WIDTH = NSA_HEADS * HEAD_DIM
GDN_WIDTH = GDN_HEADS * HEAD_DIM
SWA_WIDTH = SWA_HEADS * HEAD_DIM
SWA_KV_WIDTH = SWA_KV_HEADS * HEAD_DIM
MIX_WIDTH = NSA_WIDTH + GDN_WIDTH + SWA_WIDTH

CMP_BLOCK = 32
CMP_STRIDE = 16
CMP_HIDDEN = 2 * HEAD_DIM
SEL_BLOCK = 64
SEL_TOPN = 8
NSA_WINDOW = 512
FORCE_BONUS = 1000.0

GDN_CONV = 4
GDN_CHUNK = 64

SWA_WINDOW = 128
Q_BLOCK = 128

D_FF = 7168
N_EXPERTS = 8
TOP_K = 2
EXPERT_FF = 7168
MOE_BLOCK = 512
N_DENSE = (DEPTH + 1) // 2
N_MOE = DEPTH // 2

NORM_EPS = 1e-6
NEG_INF = -1e30

IN_SPLIT_SIZES = (NSA_WIDTH, HEAD_DIM, HEAD_DIM, HEAD_DIM, HEAD_DIM, HEAD_DIM, HEAD_DIM, 3 * NSA_HEADS,
                  GDN_WIDTH, GDN_WIDTH, GDN_WIDTH, GDN_WIDTH, GDN_HEADS, GDN_HEADS,
                  SWA_WIDTH, SWA_KV_WIDTH, SWA_KV_WIDTH)
D_IN = sum(IN_SPLIT_SIZES)

kernel_name = "hybrid_nsa_gdn_swa_moe_trunk"


def rmsnorm(x, w):
    xf = x.astype(jnp.float32)
    y = xf * lax.rsqrt(jnp.mean(xf * xf, axis=-1, keepdims=True) + NORM_EPS)
    return (y * w.astype(jnp.float32)).astype(x.dtype)


def l2norm(t):
    return t * lax.rsqrt(jnp.sum(t * t, axis=-1, keepdims=True) + NORM_EPS)


def masked_softmax(logits, mask):
    return jax.nn.softmax(jnp.where(mask, logits, NEG_INF), axis=-1)


def alibi_slopes():
    i = jnp.arange(1, N_SOFTMAX_HEADS + 1, dtype=jnp.float32)
    s = 2.0 ** (-8.0 * i / N_SOFTMAX_HEADS)
    return s[SWA_HEADS:], s[:SWA_HEADS]


def split_cols(t, sizes):
    out, start = [], 0
    for s in sizes:
        out.append(t[..., start:start + s])
        start += s
    return out


def swiglu(h, w_gate, w_up, w_down):
    return jnp.dot(jax.nn.silu(jnp.dot(h, w_gate)) * jnp.dot(h, w_up), w_down)


def banded_attention(q, k, v, slopes, window, sinks=None):
    B, S, H, d = q.shape
    G = k.shape[2]
    R = H // G
    nq = S // Q_BLOCK
    n_prev = -(-(window - 1) // Q_BLOCK)

    def band(t):
        tb = t.reshape(B, nq, Q_BLOCK, G, d)
        tb = jnp.pad(tb, ((0, 0), (n_prev, 0), (0, 0), (0, 0), (0, 0)))
        return jnp.concatenate([tb[:, i:i + nq] for i in range(n_prev + 1)], axis=2)

    kb, vb = band(k), band(v)
    qb = q.reshape(B, nq, Q_BLOCK, G, R, d)
    s = jnp.einsum('bnqgrd,bnkgd->bgrnqk', qb, kb).astype(jnp.float32) * (d ** -0.5)
    qpos = jnp.arange(nq)[:, None] * Q_BLOCK + jnp.arange(Q_BLOCK)[None]
    kpos = jnp.arange(nq)[:, None] * Q_BLOCK - n_prev * Q_BLOCK + jnp.arange((n_prev + 1) * Q_BLOCK)[None]
    dist = qpos[:, :, None] - kpos[:, None, :]
    mask = (dist >= 0) & (dist < window) & (kpos[:, None, :] >= 0)
    logits = s - slopes.astype(jnp.float32).reshape(G, R, 1, 1, 1) * dist.astype(jnp.float32)
    logits = jnp.where(mask, logits, NEG_INF)
    if sinks is not None:
        sink = jnp.broadcast_to(sinks.astype(jnp.float32).reshape(G, R, 1, 1, 1), logits.shape[:-1] + (1,))
        p = jax.nn.softmax(jnp.concatenate([logits, sink], axis=-1), axis=-1)[..., :-1]
    else:
        p = jax.nn.softmax(logits, axis=-1)
    o = jnp.einsum('bgrnqk,bnkgd->bnqgrd', p.astype(v.dtype), vb)
    return o.reshape(B, S, H, d)


def nsa_mixer(q, k_cmp, v_cmp, k_slc, v_slc, k_win, v_win, gate_logits,
              pos_k, pos_v, w1_k, w2_k, w1_v, w2_v, slopes):
    B, S, H, d = q.shape
    f32 = jnp.float32
    scale = d ** -0.5
    t_pos = jnp.arange(S)

    n_cmp = (S - CMP_BLOCK) // CMP_STRIDE + 1
    win_idx = jnp.arange(n_cmp)[:, None] * CMP_STRIDE + jnp.arange(CMP_BLOCK)[None]

    def compress(t, pos, w1, w2):
        blocks = (t[:, win_idx] + pos).reshape(B, n_cmp, CMP_BLOCK * d)
        return jnp.dot(jax.nn.silu(jnp.dot(blocks, w1)), w2)

    kc = compress(k_cmp, pos_k, w1_k, w2_k)
    vc = compress(v_cmp, pos_v, w1_v, w2_v)
    c_end = jnp.arange(n_cmp) * CMP_STRIDE + CMP_BLOCK - 1
    c_dist = (t_pos[:, None] - c_end[None]).astype(f32)
    sc = jnp.einsum('bthd,bnd->bhtn', q, kc).astype(f32) * scale - slopes[:, None, None] * c_dist
    p_cmp = masked_softmax(sc, c_dist >= 0) * (t_pos >= CMP_BLOCK - 1).astype(f32)[:, None]
    o_cmp = jnp.einsum('bhtn,bnd->bthd', p_cmp.astype(vc.dtype), vc)

    n_sel = S // SEL_BLOCK
    top_n = min(SEL_TOPN, n_sel)
    c_start = jnp.arange(n_cmp) * CMP_STRIDE
    s_start = jnp.arange(n_sel) * SEL_BLOCK
    overlap = jnp.clip(jnp.minimum(c_start[:, None] + CMP_BLOCK, s_start[None] + SEL_BLOCK)
                       - jnp.maximum(c_start[:, None], s_start[None]), 0).astype(f32) / CMP_BLOCK
    importance = jnp.einsum('bhtn,nj->btj', p_cmp, overlap)
    cur = t_pos // SEL_BLOCK
    blk = jnp.arange(n_sel)
    valid = blk[None] <= cur[:, None]
    forced = (blk[None] == 0) | (blk[None] == cur[:, None]) | (blk[None] == cur[:, None] - 1)
    score = jnp.where(valid, importance + jnp.where(forced, FORCE_BONUS, 0.0), -1.0)
    _, sel = lax.top_k(score, top_n)
    tok = (sel[..., None] * SEL_BLOCK + jnp.arange(SEL_BLOCK)).reshape(B, S, top_n * SEL_BLOCK)

    nq = S // Q_BLOCK
    qb = q.reshape(B, nq, Q_BLOCK, H, d).transpose(1, 0, 2, 3, 4)
    tokb = tok.reshape(B, nq, Q_BLOCK, top_n * SEL_BLOCK).transpose(1, 0, 2, 3)
    posb = t_pos.reshape(nq, Q_BLOCK)
    bidx = jnp.arange(B)[:, None, None]

    def sel_block(args):
        qi, ti, pi = args
        kg = k_slc[bidx, ti]
        vg = v_slc[bidx, ti]
        dist = (pi[None, :, None] - ti).astype(f32)
        s = jnp.einsum('bqhd,bqkd->bhqk', qi, kg).astype(f32) * scale - slopes[None, :, None, None] * dist[:, None]
        p = masked_softmax(s, (dist >= 0)[:, None])
        return jnp.einsum('bhqk,bqkd->bqhd', p.astype(vg.dtype), vg)

    o_slc = lax.map(sel_block, (qb, tokb, posb)).transpose(1, 0, 2, 3, 4).reshape(B, S, H, d)

    o_win = banded_attention(q, k_win[:, :, None], v_win[:, :, None], slopes, NSA_WINDOW)

    g = jax.nn.sigmoid(gate_logits.astype(f32)).reshape(B, S, 3, H)
    o = g[:, :, 0, :, None] * o_cmp + g[:, :, 1, :, None] * o_slc + g[:, :, 2, :, None] * o_win
    return o.astype(q.dtype).reshape(B, S, H * d)


def chunk_gated_delta(q, k, v, beta, g):
    B, H, S, d = q.shape
    C = GDN_CHUNK
    N = S // C
    q, k, v = (t.reshape(B, H, N, C, d) for t in (q, k, v))
    beta = beta.reshape(B, H, N, C)
    gc = jnp.cumsum(g.reshape(B, H, N, C), axis=-1)
    i = jnp.arange(C)
    lower_incl = i[:, None] >= i[None]
    strict = i[:, None] > i[None]
    decay = jnp.exp(jnp.where(lower_incl, gc[..., :, None] - gc[..., None, :], NEG_INF))
    kb = k * beta[..., None]
    L = jnp.where(strict, jnp.einsum('bhncd,bhnjd->bhncj', kb, k) * decay, 0.0)
    a_mat = L + jnp.eye(C, dtype=jnp.float32)
    rhs = jnp.concatenate([v * beta[..., None], kb * jnp.exp(gc)[..., None]], axis=-1)
    sol = lax.linalg.triangular_solve(a_mat, rhs, left_side=True, lower=True, unit_diagonal=True)
    u, w = sol[..., :d], sol[..., d:]
    attn = jnp.where(lower_incl, jnp.einsum('bhncd,bhnjd->bhncj', q, k) * decay, 0.0)
    q_dec = q * jnp.exp(gc)[..., None]
    g_last = gc[..., -1]
    k_dec = k * jnp.exp(g_last[..., None] - gc)[..., None]

    def step(state, inp):
        u_c, w_c, q_c, k_c, a_c, gl = inp
        v_new = u_c - jnp.einsum('bhck,bhkv->bhcv', w_c, state)
        o_c = jnp.einsum('bhck,bhkv->bhcv', q_c, state) + jnp.einsum('bhcj,bhjv->bhcv', a_c, v_new)
        state = state * jnp.exp(gl)[..., None, None] + jnp.einsum('bhck,bhcv->bhkv', k_c, v_new)
        return state, o_c

    xs = tuple(jnp.moveaxis(t, 2, 0) for t in (u, w, q_dec, k_dec, attn, g_last))
    state0 = jnp.zeros((B, H, d, d), jnp.float32)
    _, o = lax.scan(step, state0, xs)
    return jnp.moveaxis(o, 0, 2).reshape(B, H, S, d)


def gdn_mixer(q, k, v, z, a, b, conv_w, a_log, dt_bias, norm_w):
    B, S, _ = q.shape
    H, d = GDN_HEADS, HEAD_DIM
    f32 = jnp.float32
    qkv = jnp.concatenate([q, k, v], axis=-1)
    ch = qkv.shape[-1]
    qkv = lax.conv_general_dilated(qkv, conv_w[:, None, :].astype(qkv.dtype), window_strides=(1,),
                                   padding=[(GDN_CONV - 1, 0)], dimension_numbers=('NWC', 'WIO', 'NWC'),
                                   feature_group_count=ch)
    qkv = jax.nn.silu(qkv).astype(f32)
    qc, kc, vc = jnp.split(qkv, 3, axis=-1)

    def heads(t):
        return t.reshape(B, S, H, d).transpose(0, 2, 1, 3)

    qh = l2norm(heads(qc)) * (d ** -0.5)
    kh = l2norm(heads(kc))
    vh = heads(vc)
    beta = jax.nn.sigmoid(b.astype(f32)).transpose(0, 2, 1)
    g = -(jnp.exp(a_log.astype(f32)) * jax.nn.softplus(a.astype(f32) + dt_bias.astype(f32))).transpose(0, 2, 1)
    o = chunk_gated_delta(qh, kh, vh, beta, g).transpose(0, 2, 1, 3)
    o = o * lax.rsqrt(jnp.mean(o * o, axis=-1, keepdims=True) + NORM_EPS) * norm_w.astype(f32)
    o = o * jax.nn.silu(z.reshape(B, S, H, d).astype(f32))
    return o.reshape(B, S, H * d).astype(z.dtype)


def hybrid_mixer(h, w_in, pos_k, pos_v, w1_k, w2_k, w1_v, w2_v, conv_w, a_log, dt_bias,
                 gdn_norm_w, sinks, w_out, nsa_slopes, swa_slopes):
    B, S, _ = h.shape
    (nsa_q, kc, vc, ksl, vsl, kw, vw, nsa_g, gq, gk, gv, gz, ga, gb, sq, sk, sv) = split_cols(jnp.dot(h, w_in), IN_SPLIT_SIZES)
    o_nsa = nsa_mixer(nsa_q.reshape(B, S, NSA_HEADS, HEAD_DIM), kc, vc, ksl, vsl, kw, vw, nsa_g,
                      pos_k, pos_v, w1_k, w2_k, w1_v, w2_v, nsa_slopes)
    o_gdn = gdn_mixer(gq, gk, gv, gz, ga, gb, conv_w, a_log, dt_bias, gdn_norm_w)
    o_swa = banded_attention(sq.reshape(B, S, SWA_HEADS, HEAD_DIM), sk.reshape(B, S, SWA_KV_HEADS, HEAD_DIM),
                             sv.reshape(B, S, SWA_KV_HEADS, HEAD_DIM), swa_slopes, SWA_WINDOW, sinks).reshape(B, S, SWA_WIDTH)
    return jnp.dot(jnp.concatenate([o_nsa, o_gdn, o_swa], axis=-1), w_out)


def moe_swiglu(h, router, w_gate, w_up, w_down):
    N, D = h.shape
    logits = jnp.dot(h, router).astype(jnp.float32)
    top_logit, top_idx = lax.top_k(logits, TOP_K)
    gate = jax.nn.softmax(top_logit, axis=-1)
    A = N * TOP_K
    e_flat = top_idx.reshape(A)
    tok_flat = jnp.arange(A, dtype=jnp.int32) // TOP_K
    gate_flat = gate.reshape(A)
    order = jnp.argsort(e_flat)
    se, stok, sgate = e_flat[order], tok_flat[order], gate_flat[order]
    counts = jnp.zeros((N_EXPERTS,), jnp.int32).at[e_flat].add(1)
    starts = jnp.cumsum(counts) - counts
    padded = (counts + MOE_BLOCK - 1) // MOE_BLOCK * MOE_BLOCK
    pad_end = jnp.cumsum(padded)
    pad_start = pad_end - padded
    dest = pad_start[se] + jnp.arange(A, dtype=jnp.int32) - starts[se]
    n_blocks = -(-A // MOE_BLOCK) + N_EXPERTS
    P = n_blocks * MOE_BLOCK
    row_tok = jnp.zeros((P,), jnp.int32).at[dest].set(stok)
    row_gate = jnp.zeros((P,), jnp.float32).at[dest].set(sgate)
    blk_expert = jnp.clip(jnp.searchsorted(pad_end, jnp.arange(n_blocks, dtype=jnp.int32) * MOE_BLOCK, side='right'),
                          0, N_EXPERTS - 1)
    xs = h[row_tok].reshape(n_blocks, MOE_BLOCK, D)

    def expert_block(args):
        xb, e = args
        return swiglu(xb, w_gate[e], w_up[e], w_down[e])

    ys = lax.map(expert_block, (xs, blk_expert)).reshape(P, D)
    ys = ys * row_gate[:, None].astype(ys.dtype)
    return jax.ops.segment_sum(ys, row_tok, num_segments=N)


def setup_inputs(seed: int = 0) -> dict:
    key = jax.random.key(seed)
    ks = jax.random.split(key, 24)
    f32 = jnp.float32

    def nrm(k, shape, fan_in):
        return jax.random.normal(k, shape, f32) * (fan_in ** -0.5)

    def gain(k, shape):
        return 1.0 + 0.01 * jax.random.normal(k, shape, f32)

    dt = jnp.exp(jax.random.uniform(ks[11], (DEPTH, GDN_HEADS), f32, math.log(1e-3), math.log(1e-1)))
    return {
        "x": jax.random.normal(ks[0], (BATCH, SEQ, D_MODEL), f32),
        "attn_norm": gain(ks[1], (DEPTH, D_MODEL)),
        "w_in": nrm(ks[2], (DEPTH, D_MODEL, D_IN), D_MODEL),
        "cmp_pos_k": 0.1 * jax.random.normal(ks[3], (DEPTH, CMP_BLOCK, HEAD_DIM), f32),
        "cmp_pos_v": 0.1 * jax.random.normal(ks[4], (DEPTH, CMP_BLOCK, HEAD_DIM), f32),
        "cmp_w1_k": nrm(ks[5], (DEPTH, CMP_BLOCK * HEAD_DIM, CMP_HIDDEN), CMP_BLOCK * HEAD_DIM),
        "cmp_w2_k": nrm(ks[6], (DEPTH, CMP_HIDDEN, HEAD_DIM), CMP_HIDDEN),
        "cmp_w1_v": nrm(ks[7], (DEPTH, CMP_BLOCK * HEAD_DIM, CMP_HIDDEN), CMP_BLOCK * HEAD_DIM),
        "cmp_w2_v": nrm(ks[8], (DEPTH, CMP_HIDDEN, HEAD_DIM), CMP_HIDDEN),
        "gdn_conv_w": nrm(ks[9], (DEPTH, GDN_CONV, 3 * GDN_WIDTH), GDN_CONV),
        "gdn_a_log": jnp.log(jax.random.uniform(ks[10], (DEPTH, GDN_HEADS), f32, 1.0, 16.0)),
        "gdn_dt_bias": dt + jnp.log(-jnp.expm1(-dt)),
        "gdn_norm_w": gain(ks[12], (DEPTH, HEAD_DIM)),
        "swa_sinks": 0.5 * jax.random.normal(ks[13], (DEPTH, SWA_HEADS), f32),
        "w_out": nrm(ks[14], (DEPTH, MIX_WIDTH, D_MODEL), MIX_WIDTH),
        "ffn_norm": gain(ks[15], (DEPTH, D_MODEL)),
        "dense_w_gate": nrm(ks[16], (N_DENSE, D_MODEL, D_FF), D_MODEL),
        "dense_w_up": nrm(ks[17], (N_DENSE, D_MODEL, D_FF), D_MODEL),
        "dense_w_down": nrm(ks[18], (N_DENSE, D_FF, D_MODEL), D_FF),
        "moe_router": nrm(ks[19], (N_MOE, D_MODEL, N_EXPERTS), D_MODEL),
        "moe_w_gate": nrm(ks[20], (N_MOE, N_EXPERTS, D_MODEL, EXPERT_FF), D_MODEL),
        "moe_w_up": nrm(ks[21], (N_MOE, N_EXPERTS, D_MODEL, EXPERT_FF), D_MODEL),
        "moe_w_down": nrm(ks[22], (N_MOE, N_EXPERTS, EXPERT_FF, D_MODEL), EXPERT_FF),
        "final_norm": gain(ks[23], (D_MODEL,)),
    }


def reference(x, attn_norm, w_in, cmp_pos_k, cmp_pos_v, cmp_w1_k, cmp_w2_k, cmp_w1_v, cmp_w2_v,
              gdn_conv_w, gdn_a_log, gdn_dt_bias, gdn_norm_w, swa_sinks, w_out, ffn_norm,
              dense_w_gate, dense_w_up, dense_w_down, moe_router, moe_w_gate, moe_w_up, moe_w_down,
              final_norm):
    nsa_slopes, swa_slopes = alibi_slopes()
    B, S, D = x.shape
    for layer in range(DEPTH):
        h = rmsnorm(x, attn_norm[layer])
        x = x + hybrid_mixer(h, w_in[layer], cmp_pos_k[layer], cmp_pos_v[layer], cmp_w1_k[layer], cmp_w2_k[layer],
                             cmp_w1_v[layer], cmp_w2_v[layer], gdn_conv_w[layer], gdn_a_log[layer],
                             gdn_dt_bias[layer], gdn_norm_w[layer], swa_sinks[layer], w_out[layer],
                             nsa_slopes, swa_slopes)
        h = rmsnorm(x, ffn_norm[layer]).reshape(B * S, D)
        i = layer // 2
        if layer % 2 == 0:
            f = swiglu(h, dense_w_gate[i], dense_w_up[i], dense_w_down[i])
        else:
            f = moe_swiglu(h, moe_router[i], moe_w_gate[i], moe_w_up[i], moe_w_down[i])
        x = x + f.reshape(B, S, D)
    return rmsnorm(x, final_norm)
```

```python
import functools

import jax
import jax.numpy as jnp
import numpy as np
from jax import lax
from jax.experimental import pallas as pl
from jax.experimental.pallas import tpu as pltpu

F32 = jnp.float32
BF16 = jnp.bfloat16

D_MODEL = 2048
HEAD_DIM = 128
NSA_HEADS = 4
SWA_HEADS = 4
GDN_HEADS = 8
SWA_KV_HEADS = 2
NSA_WIDTH = NSA_HEADS * HEAD_DIM
GDN_WIDTH = GDN_HEADS * HEAD_DIM
SWA_WIDTH = SWA_HEADS * HEAD_DIM
SWA_KV_WIDTH = SWA_KV_HEADS * HEAD_DIM

CMP_BLOCK = 32
CMP_STRIDE = 16
SEL_BLOCK = 64
SEL_TOPN = 8
NSA_WINDOW = 512
FORCE_BONUS = 1000.0
GDN_CONV = 4
GDN_CHUNK = 64
SWA_WINDOW = 128
Q_BLOCK = 128
N_EXPERTS = 8
TOP_K = 2
NORM_EPS = 1e-6
NEG_INF = -1e30
ATTN_SCALE = HEAD_DIM ** -0.5

LANES = 128
VMEM_LIMIT_BYTES = 56 * 1024 * 1024

COL_NSA_Q = 0
COL_SWA_Q = 512
COL_GDN = 1024
COL_NSA_KV = COL_GDN + 4 * GDN_WIDTH
COL_SWA_K = COL_NSA_KV + 6 * HEAD_DIM
COL_SWA_V = COL_SWA_K + SWA_KV_WIDTH
COL_SMALL = COL_SWA_V + SWA_KV_WIDTH
PROJ_COLS = 6656
PROJ_TN = 1664
SMALL_A = 3 * NSA_HEADS
SMALL_B = SMALL_A + GDN_HEADS


def _alibi_slopes():
    n = NSA_HEADS + SWA_HEADS
    s = [2.0 ** (-8.0 * i / n) for i in range(1, n + 1)]
    return tuple(s[SWA_HEADS:]), tuple(s[:SWA_HEADS])


NSA_SLOPES, SWA_SLOPES = _alibi_slopes()


def _div(v, c):
    assert c & (c - 1) == 0
    return v >> (c.bit_length() - 1)


def _mod(v, c):
    assert c & (c - 1) == 0
    return v & (c - 1)


def _cparams(sem):
    return pltpu.CompilerParams(dimension_semantics=sem, vmem_limit_bytes=VMEM_LIMIT_BYTES)


def _rms(x, w):
    return x * lax.rsqrt(jnp.mean(x * x, axis=-1, keepdims=True) + NORM_EPS) * w


def _silu(x):
    return x * jax.nn.sigmoid(x)


def _dot(a, b):
    return jnp.dot(a, b, preferred_element_type=F32)


def _dot_nt(a, b):
    return lax.dot_general(a, b, (((1,), (1,)), ((), ())), preferred_element_type=F32)


def _split2(a):
    hi = a.astype(BF16)
    lo = (a - hi.astype(F32)).astype(BF16)
    return hi, lo


def _dot_hi(a, b):
    ah, al = _split2(a)
    bh, bl = _split2(b)
    return _dot(ah, bh) + (_dot(ah, bl) + _dot(al, bh))


def _dot_exact_rhs(a, b_bf16):
    a0 = a.astype(BF16)
    r = a - a0.astype(F32)
    a1 = r.astype(BF16)
    a2 = (r - a1.astype(F32)).astype(BF16)
    return _dot(a0, b_bf16) + (_dot(a1, b_bf16) + _dot(a2, b_bf16))


def _inproj_body(x_ref, nw_ref, w_ref, o_ref, h_ref):
    @pl.when(pl.program_id(1) == 0)
    def _():
        h_ref[...] = _rms(x_ref[...], nw_ref[...]).astype(BF16)

    o_ref[...] = _dot(h_ref[...], w_ref[...])


def _in_proj(x2d, norm_w, w_bf16, tm=512):
    n, d = x2d.shape
    c = w_bf16.shape[1]
    return pl.pallas_call(
        _inproj_body,
        grid=(n // tm, c // PROJ_TN),
        in_specs=[
            pl.BlockSpec((tm, d), lambda i, j: (i, 0)),
            pl.BlockSpec((1, d), lambda i, j: (0, 0)),
            pl.BlockSpec((d, PROJ_TN), lambda i, j: (0, j)),
        ],
        out_specs=pl.BlockSpec((tm, PROJ_TN), lambda i, j: (i, j)),
        out_shape=jax.ShapeDtypeStruct((n, c), F32),
        scratch_shapes=[pltpu.VMEM((tm, d), BF16)],
        compiler_params=_cparams(("parallel", "arbitrary")),
        name="in_proj",
    )(x2d, norm_w.reshape(1, d), w_bf16)


def _reorder_w_in(w):
    o = 0
    parts = {}
    for name, size in (("nsa_q", NSA_WIDTH), ("nsa_kv", 6 * HEAD_DIM), ("nsa_g", 3 * NSA_HEADS),
                       ("gdn", 4 * GDN_WIDTH), ("ga", GDN_HEADS), ("gb", GDN_HEADS),
                       ("swa_q", SWA_WIDTH), ("swa_k", SWA_KV_WIDTH), ("swa_v", SWA_KV_WIDTH)):
        parts[name] = w[:, o:o + size]
        o += size
    assert o == w.shape[1]
    pad = jnp.zeros((w.shape[0], PROJ_COLS - COL_SMALL - SMALL_B - GDN_HEADS), w.dtype)
    out = jnp.concatenate([parts["nsa_q"], parts["swa_q"], parts["gdn"], parts["nsa_kv"], parts["swa_k"],
                           parts["swa_v"], parts["nsa_g"], parts["ga"], parts["gb"], pad], axis=1)
    assert out.shape[1] == PROJ_COLS
    return out.astype(BF16)


def _flash(qs, get_k, get_v, lo, hi, t_col, slope_col, mask_fn):
    r = qs.shape[0]
    lane = lax.broadcasted_iota(jnp.int32, (r, LANES), 1)

    def body(kt, carry):
        m, l, acc = carry
        k = get_k(kt).astype(BF16)
        v = get_v(kt).astype(BF16)
        s = _dot_nt(qs, k) * ATTN_SCALE
        dist = t_col - (kt * LANES + lane)
        mask = mask_fn(kt, dist)
        logits = jnp.where(mask, s - slope_col * dist.astype(F32), NEG_INF)
        m_new = jnp.maximum(m, jnp.max(logits, axis=-1, keepdims=True))
        alpha = jnp.exp(m - m_new)
        p = jnp.where(mask, jnp.exp(logits - m_new), 0.0)
        l = alpha * l + jnp.sum(p, axis=-1, keepdims=True)
        acc = alpha * acc + _dot(p.astype(BF16), v)
        return m_new, l, acc

    init = (jnp.full((r, 1), NEG_INF, F32), jnp.zeros((r, 1), F32), jnp.zeros((r, HEAD_DIM), F32))
    return lax.fori_loop(lo, hi, body, init)


def _stack_heads(x, heads):
    return jnp.concatenate([x[:, h * HEAD_DIM:(h + 1) * HEAD_DIM] for h in heads], axis=0)


def _head_const_col(rows, values):
    head = _div(lax.broadcasted_iota(jnp.int32, (rows, 1), 0), Q_BLOCK)
    col = jnp.full((rows, 1), values[-1], F32)
    for h in range(len(values) - 2, -1, -1):
        col = jnp.where(head == h, values[h], col)
    return col


def _compress_body(ck_ref, cv_ref, pk_ref, pv_ref, w1k_ref, w2k_ref, w1v_ref, w2v_ref, kc_ref, vc_ref):
    half = CMP_STRIDE * HEAD_DIM

    def run(c_ref, p_ref, w1_ref, w2_ref, o_ref):
        c = c_ref[0]
        lo = _dot((c + p_ref[0:1, :]).astype(BF16), w1_ref[0:half, :].astype(BF16))
        hi = _dot((c + p_ref[1:2, :]).astype(BF16), w1_ref[half:2 * half, :].astype(BF16))
        hid = lo + pltpu.roll(hi, hi.shape[0] - 1, axis=0)
        o_ref[0] = _dot(_silu(hid).astype(BF16), w2_ref[...].astype(BF16))

    run(ck_ref, pk_ref, w1k_ref, w2k_ref, kc_ref)
    run(cv_ref, pv_ref, w1v_ref, w2v_ref, vc_ref)


def _nsa_compress(ck, cv, pos_k, pos_v, w1k, w2k, w1v, w2v):
    b, ns, wid = ck.shape
    hid = w1k.shape[1]
    full = lambda shape: pl.BlockSpec(shape, lambda i: (0,) * len(shape))
    return pl.pallas_call(
        _compress_body,
        grid=(b,),
        in_specs=[
            pl.BlockSpec((1, ns, wid), lambda i: (i, 0, 0)),
            pl.BlockSpec((1, ns, wid), lambda i: (i, 0, 0)),
            full((2, wid)), full((2, wid)),
            full((2 * wid, hid)), full((hid, HEAD_DIM)),
            full((2 * wid, hid)), full((hid, HEAD_DIM)),
        ],
        out_specs=[pl.BlockSpec((1, ns, HEAD_DIM), lambda i: (i, 0, 0))] * 2,
        out_shape=[jax.ShapeDtypeStruct((b, ns, HEAD_DIM), F32)] * 2,
        compiler_params=_cparams(("parallel",)),
        name="nsa_compress",
    )(ck, cv, pos_k.reshape(2, wid), pos_v.reshape(2, wid), w1k, w2k, w1v, w2v)


def _nsa_body(q_ref, gl_ref, kc_ref, vc_ref, ksl_ref, vsl_ref, kw_ref, vw_ref, o_ref, *, n_sel):
    qi = pl.program_id(1)
    h4 = range(NSA_HEADS)
    rows = NSA_HEADS * Q_BLOCK
    qs = _stack_heads(q_ref[0], h4).astype(BF16)
    row = lax.broadcasted_iota(jnp.int32, (rows, 1), 0)
    t_col = qi * Q_BLOCK + _mod(row, Q_BLOCK)
    slope_col = _head_const_col(rows, NSA_SLOPES)

    lane = lax.broadcasted_iota(jnp.int32, (rows, LANES), 1)
    c_dist = t_col - (lane * CMP_STRIDE + CMP_BLOCK - 1)
    c_mask = c_dist >= 0
    sc = _dot_nt(qs, kc_ref[0].astype(BF16)) * ATTN_SCALE
    logits = jnp.where(c_mask, sc - slope_col * c_dist.astype(F32), NEG_INF)
    e = jnp.exp(logits - jnp.max(logits, axis=-1, keepdims=True))
    p = e / jnp.sum(e, axis=-1, keepdims=True)
    p = p * (t_col >= CMP_BLOCK - 1).astype(F32)
    o_cmp = _dot(p.astype(BF16), vc_ref[0].astype(BF16))

    psum = p[0:Q_BLOCK] + p[Q_BLOCK:2 * Q_BLOCK] + p[2 * Q_BLOCK:3 * Q_BLOCK] + p[3 * Q_BLOCK:4 * Q_BLOCK]
    cn = lax.broadcasted_iota(jnp.int32, (LANES, LANES), 0) * CMP_STRIDE
    sj = lax.broadcasted_iota(jnp.int32, (LANES, LANES), 1) * SEL_BLOCK
    ov = jnp.maximum(jnp.minimum(cn + CMP_BLOCK, sj + SEL_BLOCK) - jnp.maximum(cn, sj), 0)
    ov = (ov.astype(F32) * (1.0 / CMP_BLOCK)).astype(BF16)
    imp = _dot_exact_rhs(psum, ov)
    blk = lax.broadcasted_iota(jnp.int32, (Q_BLOCK, LANES), 1)
    blk_f = blk.astype(F32)
    tq = qi * Q_BLOCK + lax.broadcasted_iota(jnp.int32, (Q_BLOCK, 1), 0)
    cur = _div(tq, SEL_BLOCK)
    forced = (blk == 0) | (blk == cur) | (blk == cur - 1)
    score = jnp.where(blk <= cur, imp + jnp.where(forced, FORCE_BONUS, 0.0), -1.0)
    score = jnp.where(blk < n_sel, score, -2.0)
    sel = jnp.zeros((Q_BLOCK, LANES), F32)
    for _ in range(SEL_TOPN):
        best = jnp.max(score, axis=-1, keepdims=True)
        idx = jnp.min(jnp.where(score == best, blk_f, float(LANES)), axis=-1, keepdims=True)
        hit = blk_f == idx
        sel = jnp.where(hit, 1.0, sel)
        score = jnp.where(hit, -3.0, score)
    sel_b = sel.astype(BF16)

    ej = lax.broadcasted_iota(jnp.int32, (LANES, LANES), 0)
    ec = _div(lax.broadcasted_iota(jnp.int32, (LANES, LANES), 1), SEL_BLOCK)
    per_tile = LANES // SEL_BLOCK

    def sel_mask(kt, dist):
        expand = jnp.where(ej == kt * per_tile + ec, 1.0, 0.0).astype(BF16)
        member = _dot(sel_b, expand)
        member = jnp.concatenate([member] * NSA_HEADS, axis=0)
        return (member > 0.5) & (dist >= 0)

    def tile(ref):
        return lambda kt: ref[0, pl.ds(pl.multiple_of(kt * LANES, LANES), LANES), :]

    m, l, acc = _flash(qs, tile(ksl_ref), tile(vsl_ref), 0, qi + 1, t_col, slope_col, sel_mask)
    o_slc = acc / l

    def win_mask(kt, dist):
        return (dist >= 0) & (dist < NSA_WINDOW)

    n_prev = -(-(NSA_WINDOW - 1) // Q_BLOCK)
    m, l, acc = _flash(qs, tile(kw_ref), tile(vw_ref), jnp.maximum(qi - n_prev, 0), qi + 1, t_col, slope_col,
                       win_mask)
    o_win = acc / l

    g = jax.nn.sigmoid(gl_ref[0])

    def gate(branch):
        return jnp.concatenate([g[:, branch * NSA_HEADS + h:branch * NSA_HEADS + h + 1] for h in h4], axis=0)

    o = gate(0) * o_cmp + gate(1) * o_slc + gate(2) * o_win
    o_ref[0] = jnp.concatenate([o[h * Q_BLOCK:(h + 1) * Q_BLOCK] for h in h4], axis=1).astype(o_ref.dtype)


def _nsa_attention(proj, kc, vc):
    b, s, _ = proj.shape
    nq = s // Q_BLOCK
    kvb = COL_NSA_KV // HEAD_DIM
    seq = lambda c: pl.BlockSpec((1, s, HEAD_DIM), lambda i, j: (i, 0, c))
    return pl.pallas_call(
        functools.partial(_nsa_body, n_sel=s // SEL_BLOCK),
        grid=(b, nq),
        in_specs=[
            pl.BlockSpec((1, Q_BLOCK, NSA_WIDTH), lambda i, j: (i, j, COL_NSA_Q // NSA_WIDTH)),
            pl.BlockSpec((1, Q_BLOCK, LANES), lambda i, j: (i, j, COL_SMALL // LANES)),
            pl.BlockSpec((1, kc.shape[1], HEAD_DIM), lambda i, j: (i, 0, 0)),
            pl.BlockSpec((1, vc.shape[1], HEAD_DIM), lambda i, j: (i, 0, 0)),
            seq(kvb + 2), seq(kvb + 3), seq(kvb + 4), seq(kvb + 5),
        ],
        out_specs=pl.BlockSpec((1, Q_BLOCK, NSA_WIDTH), lambda i, j: (i, j, 0)),
        out_shape=jax.ShapeDtypeStruct((b, s, NSA_WIDTH), BF16),
        compiler_params=_cparams(("parallel", "arbitrary")),
        name="nsa_attention",
    )(proj, proj, kc, vc, proj, proj, proj, proj)


def _swa_body(sink_ref, q_ref, k_ref, v_ref, o_ref):
    qi = pl.program_id(1)
    rep = SWA_HEADS // SWA_KV_HEADS
    rows = rep * Q_BLOCK
    row = lax.broadcasted_iota(jnp.int32, (rows, 1), 0)
    t_col = qi * Q_BLOCK + _mod(row, Q_BLOCK)
    n_prev = -(-(SWA_WINDOW - 1) // Q_BLOCK)

    def win_mask(kt, dist):
        return (dist >= 0) & (dist < SWA_WINDOW)

    outs = []
    for g in range(SWA_KV_HEADS):
        heads = [g * rep + r for r in range(rep)]
        qs = _stack_heads(q_ref[0], heads).astype(BF16)
        slope_col = _head_const_col(rows, [SWA_SLOPES[h] for h in heads])
        sink_col = _head_const_col(rows, [sink_ref[h] for h in heads])

        def tile(ref, g=g):
            return lambda kt: ref[0, pl.ds(pl.multiple_of(kt * LANES, LANES), LANES),
                                  g * HEAD_DIM:(g + 1) * HEAD_DIM]

        m, l, acc = _flash(qs, tile(k_ref), tile(v_ref), jnp.maximum(qi - n_prev, 0), qi + 1, t_col,
                           slope_col, win_mask)
        m_all = jnp.maximum(m, sink_col)
        scale = jnp.exp(m - m_all)
        o = acc * scale / (l * scale + jnp.exp(sink_col - m_all))
        outs += [o[r * Q_BLOCK:(r + 1) * Q_BLOCK] for r in range(rep)]
    o_ref[0] = jnp.concatenate(outs, axis=1).astype(o_ref.dtype)


def _swa_attention(proj, sinks):
    b, s, _ = proj.shape
    nq = s // Q_BLOCK
    return pl.pallas_call(
        _swa_body,
        grid=(b, nq),
        in_specs=[
            pl.BlockSpec(memory_space=pltpu.SMEM),
            pl.BlockSpec((1, Q_BLOCK, SWA_WIDTH), lambda i, j: (i, j, COL_SWA_Q // SWA_WIDTH)),
            pl.BlockSpec((1, s, SWA_KV_WIDTH), lambda i, j: (i, 0, COL_SWA_K // SWA_KV_WIDTH)),
            pl.BlockSpec((1, s, SWA_KV_WIDTH), lambda i, j: (i, 0, COL_SWA_V // SWA_KV_WIDTH)),
        ],
        out_specs=pl.BlockSpec((1, Q_BLOCK, SWA_WIDTH), lambda i, j: (i, j, 0)),
        out_shape=jax.ShapeDtypeStruct((b, s, SWA_WIDTH), BF16),
        compiler_params=_cparams(("parallel", "arbitrary")),
        name="swa_attention",
    )(sinks, proj, proj, proj)


GDN_PAIR = 2 * GDN_CHUNK


def _gdn_body(q_ref, k_ref, v_ref, z_ref, ar_ref, ac_ref, bc_ref, cwq_ref, cwk_ref, cwv_ref, alog_ref, dtb_ref,
              nw_ref, o_ref, qn_s, kn_s, v_s, u_s, w_s, qd_s, kdt_s, at_s, eg_s, gr_s, gc_s, bc_s):
    s = q_ref.shape[1]
    n_pairs = s // GDN_PAIR
    rowi = lax.broadcasted_iota(jnp.int32, (s, 1), 0)

    def conv_silu(x, w):
        y = x * w[GDN_CONV - 1:GDN_CONV, :]
        for j in range(GDN_CONV - 1):
            sh = GDN_CONV - 1 - j
            xs = jnp.where(rowi >= sh, pltpu.roll(x, sh, axis=0), 0.0)
            y = y + xs * w[j:j + 1, :]
        return _silu(y)

    def l2n(t):
        return t * lax.rsqrt(jnp.sum(t * t, axis=-1, keepdims=True) + NORM_EPS)

    qn_s[...] = l2n(conv_silu(q_ref[0], cwq_ref[...])) * ATTN_SCALE
    kn_s[...] = l2n(conv_silu(k_ref[0], cwk_ref[...]))
    v_s[...] = conv_silu(v_ref[0], cwv_ref[...])

    a_rate = jnp.exp(alog_ref[0])
    dtb = dtb_ref[0]

    def log_decay(a):
        x = a + dtb
        return -(a_rate * (jnp.maximum(x, 0.0) + jnp.log(1.0 + jnp.exp(-jnp.abs(x)))))

    gr_s[...] = log_decay(ar_ref[0, 0])
    gc_s[...] = log_decay(ac_ref[0, 0])
    bc_s[...] = jax.nn.sigmoid(bc_ref[0, 0])

    ii = lax.broadcasted_iota(jnp.int32, (GDN_PAIR, GDN_PAIR), 0)
    jj = lax.broadcasted_iota(jnp.int32, (GDN_PAIR, GDN_PAIR), 1)
    same = _div(ii, GDN_CHUNK) == _div(jj, GDN_CHUNK)
    incl = same & (ii >= jj)
    strict = same & (ii > jj)
    incl_t = same & (jj >= ii)
    eye = jnp.where(ii == jj, 1.0, 0.0)
    last_of = same & (_mod(jj, GDN_CHUNK) == GDN_CHUNK - 1)
    n_double = int(np.log2(GDN_CHUNK)) - 1

    def prepare(p, carry):
        r0 = pl.multiple_of(p * GDN_PAIR, GDN_PAIR)
        rows = pl.ds(r0, GDN_PAIR)
        g_row = gr_s[pl.ds(p, 1), :]
        g_col = gc_s[rows, :]
        beta = bc_s[rows, :]
        gc_col = jnp.sum(jnp.where(incl, g_row, 0.0), axis=1, keepdims=True)
        gc_row = jnp.sum(jnp.where(incl_t, g_col, 0.0), axis=0, keepdims=True)
        decay = jnp.exp(jnp.where(incl, gc_col - gc_row, NEG_INF))
        gl_col = jnp.sum(jnp.where(last_of, gc_row, 0.0), axis=1, keepdims=True)
        q = qn_s[rows, :]
        k = kn_s[rows, :]
        v = v_s[rows, :]
        kb = k * beta
        kf = k.astype(BF16)
        lmat = jnp.where(strict, _dot_nt(kb.astype(BF16), kf) * decay, 0.0)
        pw = -lmat
        tinv = eye + pw
        for _ in range(n_double):
            pw = _dot_hi(pw, pw)
            tinv = tinv + _dot_hi(tinv, pw)
        egc = jnp.exp(gc_col)
        sol = _dot_hi(tinv, jnp.concatenate([v * beta, kb * egc], axis=1))
        u_s[rows, :] = sol[:, :HEAD_DIM]
        w_s[rows, :] = sol[:, HEAD_DIM:]
        at_s[rows, :] = jnp.where(incl, _dot_nt(q.astype(BF16), kf) * decay, 0.0)
        qd_s[rows, :] = q * egc
        kdt_s[rows, :] = (k * jnp.exp(gl_col - gc_col)).T
        eg_s[rows, :] = jnp.broadcast_to(jnp.exp(gl_col), (GDN_PAIR, HEAD_DIM))
        return carry

    lax.fori_loop(0, n_pairs, prepare, 0)

    rin = lax.broadcasted_iota(jnp.int32, (GDN_PAIR, 1), 0)
    nw = nw_ref[...]

    def scan(p, state):
        r0 = pl.multiple_of(p * GDN_PAIR, GDN_PAIR)
        rows = pl.ds(r0, GDN_PAIR)
        u = u_s[rows, :]
        w = w_s[rows, :].astype(BF16)
        qd = qd_s[rows, :].astype(BF16)
        at = at_s[rows, :].astype(BF16)
        kdt = kdt_s[rows, :].astype(BF16)
        eg = eg_s[rows, :]
        o = jnp.zeros((GDN_PAIR, HEAD_DIM), F32)
        for c in range(GDN_PAIR // GDN_CHUNK):
            in_c = _div(rin, GDN_CHUNK) == c
            sb = state.astype(BF16)
            v_new = jnp.where(in_c, u - _dot(w, sb), 0.0)
            vb = v_new.astype(BF16)
            o = jnp.where(in_c, _dot(qd, sb) + _dot(at, vb), o)
            state = state * eg[c * GDN_CHUNK:c * GDN_CHUNK + 1, :] + _dot(kdt, vb)
        o = o * lax.rsqrt(jnp.mean(o * o, axis=-1, keepdims=True) + NORM_EPS) * nw
        o_ref[0, rows, :] = (o * _silu(z_ref[0, rows, :])).astype(o_ref.dtype)
        return state

    lax.fori_loop(0, n_pairs, scan, jnp.zeros((HEAD_DIM, HEAD_DIM), F32))


def _gdn_mixer(proj, conv_w, a_log, dt_bias, norm_w):
    b, s, _ = proj.shape
    h = GDN_HEADS
    n_pairs = s // GDN_PAIR
    ga = proj[:, :, COL_SMALL + SMALL_A:COL_SMALL + SMALL_A + h].transpose(0, 2, 1)
    gb = proj[:, :, COL_SMALL + SMALL_B:COL_SMALL + SMALL_B + h].transpose(0, 2, 1)
    base = COL_GDN // HEAD_DIM
    seq = lambda off: pl.BlockSpec((1, s, HEAD_DIM), lambda i, j: (i, 0, base + off * h + j))
    colv = pl.BlockSpec((1, 1, s, 1), lambda i, j: (i, j, 0, 0))
    cw = lambda off: pl.BlockSpec((GDN_CONV, HEAD_DIM), lambda i, j: (0, off * h + j))
    scalar = pl.BlockSpec((1, 1, 1), lambda i, j: (j, 0, 0))
    seq_f32 = pltpu.VMEM((s, HEAD_DIM), F32)
    return pl.pallas_call(
        _gdn_body,
        grid=(b, h),
        in_specs=[
            seq(0), seq(1), seq(2), seq(3),
            pl.BlockSpec((1, 1, n_pairs, GDN_PAIR), lambda i, j: (i, j, 0, 0)),
            colv, colv,
            cw(0), cw(1), cw(2),
            scalar, scalar,
            pl.BlockSpec((1, HEAD_DIM), lambda i, j: (0, 0)),
        ],
        out_specs=pl.BlockSpec((1, s, HEAD_DIM), lambda i, j: (i, 0, j)),
        out_shape=jax.ShapeDtypeStruct((b, s, GDN_WIDTH), BF16),
        scratch_shapes=[seq_f32] * 9 + [pltpu.VMEM((n_pairs, GDN_PAIR), F32), pltpu.VMEM((s, 1), F32),
                                        pltpu.VMEM((s, 1), F32)],
        compiler_params=_cparams(("parallel", "arbitrary")),
        name="gdn_mixer",
    )(proj, proj, proj, proj, ga.reshape(b, h, n_pairs, GDN_PAIR), ga.reshape(b, h, s, 1),
      gb.reshape(b, h, s, 1), conv_w, conv_w, conv_w, a_log.reshape(h, 1, 1), dt_bias.reshape(h, 1, 1),
      norm_w.reshape(1, HEAD_DIM))


def _outproj_body(on_ref, og_ref, os_ref, x_ref, w_ref, o_ref):
    mix = jnp.concatenate([on_ref[...], og_ref[...], os_ref[...]], axis=1)
    o_ref[...] = x_ref[...] + _dot(mix, w_ref[...])


def _outproj_router_body(on_ref, og_ref, os_ref, x_ref, w_ref, nw_ref, r_ref, o_ref, h_ref, lg_ref):
    mix = jnp.concatenate([on_ref[...], og_ref[...], os_ref[...]], axis=1)
    x1 = x_ref[...] + _dot(mix, w_ref[...])
    o_ref[...] = x1
    h = _rms(x1, nw_ref[...])
    h_ref[...] = h.astype(h_ref.dtype)
    lg_ref[...] = _dot_hi(h, r_ref[...])


def _out_proj(o_nsa, o_gdn, o_swa, x2d, w_bf16, ffn_norm=None, router=None, tm=512):
    n, d = x2d.shape
    row = lambda wid: pl.BlockSpec((tm, wid), lambda i: (i, 0))
    full = lambda a, c: pl.BlockSpec((a, c), lambda i: (0, 0))
    in_specs = [row(NSA_WIDTH), row(GDN_WIDTH), row(SWA_WIDTH), row(d), full(d, d)]
    args = [o_nsa, o_gdn, o_swa, x2d, w_bf16]
    if router is None:
        return pl.pallas_call(
            _outproj_body, grid=(n // tm,), in_specs=in_specs, out_specs=row(d),
            out_shape=jax.ShapeDtypeStruct((n, d), F32),
            compiler_params=_cparams(("parallel",)), name="out_proj",
        )(*args)
    router_pad = jnp.pad(router, ((0, 0), (0, LANES - router.shape[1])))
    return pl.pallas_call(
        _outproj_router_body, grid=(n // tm,),
        in_specs=in_specs + [full(1, d), full(d, LANES)],
        out_specs=[row(d), row(d), row(LANES)],
        out_shape=[jax.ShapeDtypeStruct((n, d), F32), jax.ShapeDtypeStruct((n, d), BF16),
                   jax.ShapeDtypeStruct((n, LANES), F32)],
        compiler_params=_cparams(("parallel",)), name="out_proj_router",
    )(*args, ffn_norm.reshape(1, d), router_pad)


def _swiglu_tile(h, wg, wu, wd):
    a = _dot(h, wg.astype(BF16))
    b = _dot(h, wu.astype(BF16))
    return _dot((_silu(a) * b).astype(BF16), wd.astype(BF16))


def _dense_body(x_ref, nw_ref, wg_ref, wu_ref, wd_ref, o_ref, h_ref):
    @pl.when(pl.program_id(1) == 0)
    def _():
        x = x_ref[...]
        h_ref[...] = _rms(x, nw_ref[...]).astype(BF16)
        o_ref[...] = x

    o_ref[...] += _swiglu_tile(h_ref[...], wg_ref[...], wu_ref[...], wd_ref[...])


def _dense_ffn(x2d, norm_w, wg, wu, wd, tm=512, tf=256):
    n, d = x2d.shape
    ff = wg.shape[1]
    return pl.pallas_call(
        _dense_body,
        grid=(n // tm, ff // tf),
        in_specs=[
            pl.BlockSpec((tm, d), lambda i, f: (i, 0)),
            pl.BlockSpec((1, d), lambda i, f: (0, 0)),
            pl.BlockSpec((d, tf), lambda i, f: (0, f)),
            pl.BlockSpec((d, tf), lambda i, f: (0, f)),
            pl.BlockSpec((tf, d), lambda i, f: (f, 0)),
        ],
        out_specs=pl.BlockSpec((tm, d), lambda i, f: (i, 0)),
        out_shape=jax.ShapeDtypeStruct((n, d), F32),
        scratch_shapes=[pltpu.VMEM((tm, d), BF16)],
        compiler_params=_cparams(("parallel", "arbitrary")),
        name="dense_ffn",
    )(x2d, norm_w.reshape(1, d), wg, wu, wd)


def _moe_body(be_ref, nu_ref, x_ref, g_ref, wg_ref, wu_ref, wd_ref, o_ref):
    i = pl.program_id(0)
    f = pl.program_id(1)
    used = i < nu_ref[0]

    @pl.when(f == 0)
    def _():
        o_ref[...] = jnp.zeros_like(o_ref)

    @pl.when(used)
    def _():
        o_ref[...] += _swiglu_tile(x_ref[...], wg_ref[0], wu_ref[0], wd_ref[0])

    @pl.when(used & (f == pl.num_programs(1) - 1))
    def _():
        o_ref[...] = o_ref[...] * g_ref[...]


def _moe_ffn(xs, row_gate, blk_expert, n_used, wg, wu, wd, tm, tf=256):
    p, d = xs.shape
    ff = wg.shape[2]
    nb = p // tm
    nf = ff // tf

    def xmap(i, f, be, nu):
        return (jnp.minimum(i, nu[0] - 1), 0)

    def fsel(i, f, nu):
        return jnp.where(i < nu[0], f, nf - 1)

    grid_spec = pltpu.PrefetchScalarGridSpec(
        num_scalar_prefetch=2,
        grid=(nb, nf),
        in_specs=[
            pl.BlockSpec((tm, d), xmap),
            pl.BlockSpec((tm, 1), xmap),
            pl.BlockSpec((1, d, tf), lambda i, f, be, nu: (be[i], 0, fsel(i, f, nu))),
            pl.BlockSpec((1, d, tf), lambda i, f, be, nu: (be[i], 0, fsel(i, f, nu))),
            pl.BlockSpec((1, tf, d), lambda i, f, be, nu: (be[i], fsel(i, f, nu), 0)),
        ],
        out_specs=pl.BlockSpec((tm, d), lambda i, f, be, nu: (i, 0)),
    )
    return pl.pallas_call(
        _moe_body,
        grid_spec=grid_spec,
        out_shape=jax.ShapeDtypeStruct((p, d), F32),
        compiler_params=_cparams(("arbitrary", "arbitrary")),
        name="moe_ffn",
    )(blk_expert, n_used, xs, row_gate, wg, wu, wd)


def _moe_layer(h_bf16, logits, wg, wu, wd, tm=1024):
    n, d = h_bf16.shape
    top_logit, top_idx = lax.top_k(logits, TOP_K)
    gate = jax.nn.softmax(top_logit, axis=-1)
    a = n * TOP_K
    e_flat = top_idx.reshape(a)
    onehot = (e_flat[:, None] == jnp.arange(N_EXPERTS, dtype=e_flat.dtype)[None]).astype(jnp.int32)
    rank = jnp.sum((jnp.cumsum(onehot, axis=0) - onehot) * onehot, axis=1)
    counts = jnp.sum(onehot, axis=0)
    padded = (counts + tm - 1) // tm * tm
    pad_end = jnp.cumsum(padded)
    pad_start = pad_end - padded
    dest = pad_start[e_flat] + rank
    nb = a // tm + N_EXPERTS
    p = nb * tm
    tok = jnp.arange(a, dtype=jnp.int32) // TOP_K
    row_tok = jnp.zeros((p,), jnp.int32).at[dest].set(tok)
    row_gate = jnp.zeros((p,), F32).at[dest].set(gate.reshape(a))
    n_used = (pad_end[-1] // tm).astype(jnp.int32)
    blk = jnp.minimum(jnp.arange(nb, dtype=jnp.int32), n_used - 1)
    blk_expert = jnp.clip(jnp.searchsorted(pad_end, blk * tm, side="right"), 0, N_EXPERTS - 1).astype(jnp.int32)
    xs = h_bf16[row_tok]
    ys = _moe_ffn(xs, row_gate.reshape(p, 1), blk_expert, n_used.reshape(1), wg, wu, wd, tm)
    dest2 = dest.reshape(n, TOP_K)
    return ys[dest2[:, 0]] + ys[dest2[:, 1]]


def _final_body(x_ref, f_ref, w_ref, o_ref):
    o_ref[...] = _rms(x_ref[...] + f_ref[...], w_ref[...])


def _residual_norm(x2d, f2d, w, tm=512):
    n, d = x2d.shape
    row = pl.BlockSpec((tm, d), lambda i: (i, 0))
    return pl.pallas_call(
        _final_body, grid=(n // tm,),
        in_specs=[row, row, pl.BlockSpec((1, d), lambda i: (0, 0))],
        out_specs=row, out_shape=jax.ShapeDtypeStruct((n, d), F32),
        compiler_params=_cparams(("parallel",)), name="residual_norm",
    )(x2d, f2d, w.reshape(1, d))


def kernel(x, attn_norm, w_in, cmp_pos_k, cmp_pos_v, cmp_w1_k, cmp_w2_k, cmp_w1_v, cmp_w2_v, gdn_conv_w, gdn_a_log, gdn_dt_bias, gdn_norm_w, swa_sinks, w_out, ffn_norm, dense_w_gate, dense_w_up, dense_w_down, moe_router, moe_w_gate, moe_w_up, moe_w_down, final_norm):
    b, s, d = x.shape
    n = b * s
    depth = w_in.shape[0]
    assert depth % 2 == 0, "the trunk ends on an expert layer, whose combine feeds the final norm"
    x2d = x.reshape(n, d)
    out = None
    for layer in range(depth):
        proj = _in_proj(x2d, attn_norm[layer], _reorder_w_in(w_in[layer])).reshape(b, s, PROJ_COLS)
        n_strides = s // CMP_STRIDE
        ck = proj[:, :, COL_NSA_KV:COL_NSA_KV + HEAD_DIM].reshape(b, n_strides, CMP_STRIDE * HEAD_DIM)
        cv = proj[:, :, COL_NSA_KV + HEAD_DIM:COL_NSA_KV + 2 * HEAD_DIM].reshape(b, n_strides, CMP_STRIDE * HEAD_DIM)
        kc, vc = _nsa_compress(ck, cv, cmp_pos_k[layer], cmp_pos_v[layer], cmp_w1_k[layer], cmp_w2_k[layer],
                               cmp_w1_v[layer], cmp_w2_v[layer])
        o_nsa = _nsa_attention(proj, kc, vc).reshape(n, NSA_WIDTH)
        o_gdn = _gdn_mixer(proj, gdn_conv_w[layer], gdn_a_log[layer], gdn_dt_bias[layer],
                           gdn_norm_w[layer]).reshape(n, GDN_WIDTH)
        o_swa = _swa_attention(proj, swa_sinks[layer]).reshape(n, SWA_WIDTH)
        w_o = w_out[layer].astype(BF16)
        i = layer // 2
        if layer % 2 == 0:
            x1 = _out_proj(o_nsa, o_gdn, o_swa, x2d, w_o)
            x2d = _dense_ffn(x1, ffn_norm[layer], dense_w_gate[i], dense_w_up[i], dense_w_down[i])
        else:
            x1, h2, logits = _out_proj(o_nsa, o_gdn, o_swa, x2d, w_o, ffn_norm[layer], moe_router[i])
            f = _moe_layer(h2, logits[:, :N_EXPERTS], moe_w_gate[i], moe_w_up[i], moe_w_down[i])
            if layer == depth - 1:
                out = _residual_norm(x1, f, final_norm)
            else:
                x2d = x1 + f
    return out.reshape(b, s, d)
```

```python
import functools

import jax
import jax.numpy as jnp
import numpy as np
from jax import lax
from jax.experimental import pallas as pl
from jax.experimental.pallas import tpu as pltpu

F32 = jnp.float32
BF16 = jnp.bfloat16

D_MODEL = 2048
HEAD_DIM = 128
NSA_HEADS = 4
SWA_HEADS = 4
GDN_HEADS = 8
SWA_KV_HEADS = 2
NSA_WIDTH = NSA_HEADS * HEAD_DIM
GDN_WIDTH = GDN_HEADS * HEAD_DIM
SWA_WIDTH = SWA_HEADS * HEAD_DIM
SWA_KV_WIDTH = SWA_KV_HEADS * HEAD_DIM

CMP_BLOCK = 32
CMP_STRIDE = 16
SEL_BLOCK = 64
SEL_TOPN = 8
NSA_WINDOW = 512
FORCE_BONUS = 1000.0
GDN_CONV = 4
GDN_CHUNK = 64
SWA_WINDOW = 128
Q_BLOCK = 128
N_EXPERTS = 8
TOP_K = 2
NORM_EPS = 1e-6
NEG_INF = -1e30
ATTN_SCALE = HEAD_DIM ** -0.5

LANES = 128
VMEM_LIMIT_BYTES = 56 * 1024 * 1024

COL_NSA_Q = 0
COL_SWA_Q = 512
COL_GDN = 1024
COL_NSA_KV = COL_GDN + 4 * GDN_WIDTH
COL_SWA_K = COL_NSA_KV + 6 * HEAD_DIM
COL_SWA_V = COL_SWA_K + SWA_KV_WIDTH
COL_SMALL = COL_SWA_V + SWA_KV_WIDTH
PROJ_COLS = 6656
PROJ_TN = 1664
SMALL_A = 3 * NSA_HEADS
SMALL_B = SMALL_A + GDN_HEADS


def _alibi_slopes():
    n = NSA_HEADS + SWA_HEADS
    s = [2.0 ** (-8.0 * i / n) for i in range(1, n + 1)]
    return tuple(s[SWA_HEADS:]), tuple(s[:SWA_HEADS])


NSA_SLOPES, SWA_SLOPES = _alibi_slopes()


def _div(v, c):
    assert c & (c - 1) == 0
    return v >> (c.bit_length() - 1)


def _mod(v, c):
    assert c & (c - 1) == 0
    return v & (c - 1)


def _cparams(sem):
    return pltpu.CompilerParams(dimension_semantics=sem, vmem_limit_bytes=VMEM_LIMIT_BYTES)


def _rms(x, w):
    return x * lax.rsqrt(jnp.mean(x * x, axis=-1, keepdims=True) + NORM_EPS) * w


def _silu(x):
    return x * jax.nn.sigmoid(x)


def _dot(a, b):
    return jnp.dot(a, b, preferred_element_type=F32)


def _dot_nt(a, b):
    return lax.dot_general(a, b, (((1,), (1,)), ((), ())), preferred_element_type=F32)


def _split2(a):
    hi = a.astype(BF16)
    lo = (a - hi.astype(F32)).astype(BF16)
    return hi, lo


def _dot_hi(a, b):
    ah, al = _split2(a)
    bh, bl = _split2(b)
    return _dot(ah, bh) + (_dot(ah, bl) + _dot(al, bh))


def _dot_exact_rhs(a, b_bf16):
    a0 = a.astype(BF16)
    r = a - a0.astype(F32)
    a1 = r.astype(BF16)
    a2 = (r - a1.astype(F32)).astype(BF16)
    return _dot(a0, b_bf16) + (_dot(a1, b_bf16) + _dot(a2, b_bf16))


def _inproj_body(x_ref, nw_ref, w_ref, o_ref, h_ref):
    @pl.when(pl.program_id(1) == 0)
    def _():
        h_ref[...] = _rms(x_ref[...], nw_ref[...]).astype(BF16)

    o_ref[...] = _dot(h_ref[...], w_ref[...])


def _in_proj(x2d, norm_w, w, tm=1024):
    n, d = x2d.shape
    c = w.shape[1]
    return pl.pallas_call(
        _inproj_body,
        grid=(n // tm, c // PROJ_TN),
        in_specs=[
            pl.BlockSpec((tm, d), lambda i, j: (i, 0), pipeline_mode=pl.Buffered(1)),
            pl.BlockSpec((1, d), lambda i, j: (0, 0)),
            pl.BlockSpec((d, PROJ_TN), lambda i, j: (0, j)),
        ],
        out_specs=pl.BlockSpec((tm, PROJ_TN), lambda i, j: (i, j)),
        out_shape=jax.ShapeDtypeStruct((n, c), F32),
        scratch_shapes=[pltpu.VMEM((tm, d), BF16)],
        compiler_params=_cparams(("parallel", "arbitrary")),
        name="in_proj",
    )(x2d, norm_w.reshape(1, d), w)


def _w_in_segments():
    src = {}
    o = 0
    for name, size in (("nsa_q", NSA_WIDTH), ("nsa_kv", 6 * HEAD_DIM), ("nsa_g", 3 * NSA_HEADS),
                       ("gdn", 4 * GDN_WIDTH), ("ga", GDN_HEADS), ("gb", GDN_HEADS),
                       ("swa_q", SWA_WIDTH), ("swa_k", SWA_KV_WIDTH), ("swa_v", SWA_KV_WIDTH)):
        src[name] = (o, size)
        o += size
    dst = {"nsa_q": COL_NSA_Q, "swa_q": COL_SWA_Q, "gdn": COL_GDN, "nsa_kv": COL_NSA_KV, "swa_k": COL_SWA_K,
           "swa_v": COL_SWA_V, "nsa_g": COL_SMALL, "ga": COL_SMALL + SMALL_A, "gb": COL_SMALL + SMALL_B}
    return [(src[k][0], dst[k], src[k][1]) for k in src], o


def _reorder_body(w_ref, o_ref):
    segs, _ = _w_in_segments()
    tail = COL_SMALL + SMALL_B + GDN_HEADS
    for s0, d0, wid in segs:
        o_ref[:, d0:d0 + wid] = w_ref[0, :, s0:s0 + wid].astype(o_ref.dtype)
    o_ref[:, tail:] = jnp.zeros((o_ref.shape[0], o_ref.shape[1] - tail), o_ref.dtype)


def _reorder_w_in(w_all, layer, tr=256):
    _, d, c = w_all.shape
    assert _w_in_segments()[1] == c
    return pl.pallas_call(
        _reorder_body,
        grid=(d // tr,),
        in_specs=[pl.BlockSpec((1, tr, c), lambda i: (layer, i, 0))],
        out_specs=pl.BlockSpec((tr, PROJ_COLS), lambda i: (i, 0)),
        out_shape=jax.ShapeDtypeStruct((d, PROJ_COLS), BF16),
        compiler_params=_cparams(("parallel",)),
        name="reorder_w_in",
    )(w_all)


def _flash(qs, get_k, get_v, lo, hi, t_col, slope_col, mask_fn):
    r = qs.shape[0]
    lane = lax.broadcasted_iota(jnp.int32, (r, LANES), 1)

    def body(kt, carry):
        m, l, acc = carry
        k = get_k(kt).astype(BF16)
        v = get_v(kt).astype(BF16)
        s = _dot_nt(qs, k) * ATTN_SCALE
        dist = t_col - (kt * LANES + lane)
        mask = mask_fn(kt, dist)
        logits = jnp.where(mask, s - slope_col * dist.astype(F32), NEG_INF)
        m_new = jnp.maximum(m, jnp.max(logits, axis=-1, keepdims=True))
        alpha = jnp.exp(m - m_new)
        p = jnp.where(mask, jnp.exp(logits - m_new), 0.0)
        l = alpha * l + jnp.sum(p, axis=-1, keepdims=True)
        acc = alpha * acc + _dot(p.astype(BF16), v)
        return m_new, l, acc

    init = (jnp.full((r, 1), NEG_INF, F32), jnp.zeros((r, 1), F32), jnp.zeros((r, HEAD_DIM), F32))
    return lax.fori_loop(lo, hi, body, init)


def _stack_heads(x, heads):
    return jnp.concatenate([x[:, h * HEAD_DIM:(h + 1) * HEAD_DIM] for h in heads], axis=0)


def _head_const_col(rows, values):
    head = _div(lax.broadcasted_iota(jnp.int32, (rows, 1), 0), Q_BLOCK)
    col = jnp.full((rows, 1), values[-1], F32)
    for h in range(len(values) - 2, -1, -1):
        col = jnp.where(head == h, values[h], col)
    return col


def _compress_body(ck_ref, cv_ref, pk_ref, pv_ref, w1k_ref, w2k_ref, w1v_ref, w2v_ref, kc_ref, vc_ref):
    half = CMP_STRIDE * HEAD_DIM

    def run(c_ref, p_ref, w1_ref, w2_ref, o_ref):
        c = c_ref[0]
        lo = _dot((c + p_ref[0:1, :]).astype(BF16), w1_ref[0:half, :].astype(BF16))
        hi = _dot((c + p_ref[1:2, :]).astype(BF16), w1_ref[half:2 * half, :].astype(BF16))
        hid = lo + pltpu.roll(hi, hi.shape[0] - 1, axis=0)
        o_ref[0] = _dot(_silu(hid).astype(BF16), w2_ref[...].astype(BF16))

    run(ck_ref, pk_ref, w1k_ref, w2k_ref, kc_ref)
    run(cv_ref, pv_ref, w1v_ref, w2v_ref, vc_ref)


def _nsa_compress(ck, cv, pos_k, pos_v, w1k, w2k, w1v, w2v):
    b, ns, wid = ck.shape
    hid = w1k.shape[1]
    full = lambda shape: pl.BlockSpec(shape, lambda i: (0,) * len(shape))
    return pl.pallas_call(
        _compress_body,
        grid=(b,),
        in_specs=[
            pl.BlockSpec((1, ns, wid), lambda i: (i, 0, 0)),
            pl.BlockSpec((1, ns, wid), lambda i: (i, 0, 0)),
            full((2, wid)), full((2, wid)),
            full((2 * wid, hid)), full((hid, HEAD_DIM)),
            full((2 * wid, hid)), full((hid, HEAD_DIM)),
        ],
        out_specs=[pl.BlockSpec((1, ns, HEAD_DIM), lambda i: (i, 0, 0))] * 2,
        out_shape=[jax.ShapeDtypeStruct((b, ns, HEAD_DIM), F32)] * 2,
        compiler_params=_cparams(("parallel",)),
        name="nsa_compress",
    )(ck, cv, pos_k.reshape(2, wid), pos_v.reshape(2, wid), w1k, w2k, w1v, w2v)


def _nsa_body(q_ref, gl_ref, kc_ref, vc_ref, ksl_ref, vsl_ref, kw_ref, vw_ref, o_ref, *, n_sel):
    qi = pl.program_id(1)
    h4 = range(NSA_HEADS)
    rows = NSA_HEADS * Q_BLOCK
    qs = _stack_heads(q_ref[0], h4).astype(BF16)
    row = lax.broadcasted_iota(jnp.int32, (rows, 1), 0)
    t_col = qi * Q_BLOCK + _mod(row, Q_BLOCK)
    slope_col = _head_const_col(rows, NSA_SLOPES)

    lane = lax.broadcasted_iota(jnp.int32, (rows, LANES), 1)
    c_dist = t_col - (lane * CMP_STRIDE + CMP_BLOCK - 1)
    c_mask = c_dist >= 0
    sc = _dot_nt(qs, kc_ref[0].astype(BF16)) * ATTN_SCALE
    logits = jnp.where(c_mask, sc - slope_col * c_dist.astype(F32), NEG_INF)
    e = jnp.exp(logits - jnp.max(logits, axis=-1, keepdims=True))
    p = e / jnp.sum(e, axis=-1, keepdims=True)
    p = p * (t_col >= CMP_BLOCK - 1).astype(F32)
    o_cmp = _dot(p.astype(BF16), vc_ref[0].astype(BF16))

    psum = p[0:Q_BLOCK] + p[Q_BLOCK:2 * Q_BLOCK] + p[2 * Q_BLOCK:3 * Q_BLOCK] + p[3 * Q_BLOCK:4 * Q_BLOCK]
    cn = lax.broadcasted_iota(jnp.int32, (LANES, LANES), 0) * CMP_STRIDE
    sj = lax.broadcasted_iota(jnp.int32, (LANES, LANES), 1) * SEL_BLOCK
    ov = jnp.maximum(jnp.minimum(cn + CMP_BLOCK, sj + SEL_BLOCK) - jnp.maximum(cn, sj), 0)
    ov = (ov.astype(F32) * (1.0 / CMP_BLOCK)).astype(BF16)
    imp = _dot_exact_rhs(psum, ov)
    blk = lax.broadcasted_iota(jnp.int32, (Q_BLOCK, LANES), 1)
    blk_f = blk.astype(F32)
    tq = qi * Q_BLOCK + lax.broadcasted_iota(jnp.int32, (Q_BLOCK, 1), 0)
    cur = _div(tq, SEL_BLOCK)
    forced = (blk == 0) | (blk == cur) | (blk == cur - 1)
    score = jnp.where(blk <= cur, imp + jnp.where(forced, FORCE_BONUS, 0.0), -1.0)
    score = jnp.where(blk < n_sel, score, -2.0)
    sel = jnp.zeros((Q_BLOCK, LANES), F32)
    for _ in range(SEL_TOPN):
        best = jnp.max(score, axis=-1, keepdims=True)
        idx = jnp.min(jnp.where(score == best, blk_f, float(LANES)), axis=-1, keepdims=True)
        hit = blk_f == idx
        sel = jnp.where(hit, 1.0, sel)
        score = jnp.where(hit, -3.0, score)
    sel_b = sel.astype(BF16)

    ej = lax.broadcasted_iota(jnp.int32, (LANES, LANES), 0)
    ec = _div(lax.broadcasted_iota(jnp.int32, (LANES, LANES), 1), SEL_BLOCK)
    per_tile = LANES // SEL_BLOCK

    def sel_mask(kt, dist):
        expand = jnp.where(ej == kt * per_tile + ec, 1.0, 0.0).astype(BF16)
        member = _dot(sel_b, expand)
        member = jnp.concatenate([member] * NSA_HEADS, axis=0)
        return (member > 0.5) & (dist >= 0)

    def tile(ref):
        return lambda kt: ref[0, pl.ds(pl.multiple_of(kt * LANES, LANES), LANES), :]

    m, l, acc = _flash(qs, tile(ksl_ref), tile(vsl_ref), 0, qi + 1, t_col, slope_col, sel_mask)
    o_slc = acc / l

    def win_mask(kt, dist):
        return (dist >= 0) & (dist < NSA_WINDOW)

    n_prev = -(-(NSA_WINDOW - 1) // Q_BLOCK)
    m, l, acc = _flash(qs, tile(kw_ref), tile(vw_ref), jnp.maximum(qi - n_prev, 0), qi + 1, t_col, slope_col,
                       win_mask)
    o_win = acc / l

    g = jax.nn.sigmoid(gl_ref[0])

    def gate(branch):
        return jnp.concatenate([g[:, branch * NSA_HEADS + h:branch * NSA_HEADS + h + 1] for h in h4], axis=0)

    o = gate(0) * o_cmp + gate(1) * o_slc + gate(2) * o_win
    o_ref[0] = jnp.concatenate([o[h * Q_BLOCK:(h + 1) * Q_BLOCK] for h in h4], axis=1).astype(o_ref.dtype)


def _nsa_attention(proj, kc, vc):
    b, s, _ = proj.shape
    nq = s // Q_BLOCK
    kvb = COL_NSA_KV // HEAD_DIM
    seq = lambda c: pl.BlockSpec((1, s, HEAD_DIM), lambda i, j: (i, 0, c))
    return pl.pallas_call(
        functools.partial(_nsa_body, n_sel=s // SEL_BLOCK),
        grid=(b, nq),
        in_specs=[
            pl.BlockSpec((1, Q_BLOCK, NSA_WIDTH), lambda i, j: (i, j, COL_NSA_Q // NSA_WIDTH)),
            pl.BlockSpec((1, Q_BLOCK, LANES), lambda i, j: (i, j, COL_SMALL // LANES)),
            pl.BlockSpec((1, kc.shape[1], HEAD_DIM), lambda i, j: (i, 0, 0)),
            pl.BlockSpec((1, vc.shape[1], HEAD_DIM), lambda i, j: (i, 0, 0)),
            seq(kvb + 2), seq(kvb + 3), seq(kvb + 4), seq(kvb + 5),
        ],
        out_specs=pl.BlockSpec((1, Q_BLOCK, NSA_WIDTH), lambda i, j: (i, j, 0)),
        out_shape=jax.ShapeDtypeStruct((b, s, NSA_WIDTH), BF16),
        compiler_params=_cparams(("parallel", "arbitrary")),
        name="nsa_attention",
    )(proj, proj, kc, vc, proj, proj, proj, proj)


def _swa_body(sink_ref, q_ref, k_ref, v_ref, o_ref):
    qi = pl.program_id(1)
    rep = SWA_HEADS // SWA_KV_HEADS
    rows = rep * Q_BLOCK
    row = lax.broadcasted_iota(jnp.int32, (rows, 1), 0)
    t_col = qi * Q_BLOCK + _mod(row, Q_BLOCK)
    n_prev = -(-(SWA_WINDOW - 1) // Q_BLOCK)

    def win_mask(kt, dist):
        return (dist >= 0) & (dist < SWA_WINDOW)

    outs = []
    for g in range(SWA_KV_HEADS):
        heads = [g * rep + r for r in range(rep)]
        qs = _stack_heads(q_ref[0], heads).astype(BF16)
        slope_col = _head_const_col(rows, [SWA_SLOPES[h] for h in heads])
        sink_col = _head_const_col(rows, [sink_ref[h] for h in heads])

        def tile(ref, g=g):
            return lambda kt: ref[0, pl.ds(pl.multiple_of(kt * LANES, LANES), LANES),
                                  g * HEAD_DIM:(g + 1) * HEAD_DIM]

        m, l, acc = _flash(qs, tile(k_ref), tile(v_ref), jnp.maximum(qi - n_prev, 0), qi + 1, t_col,
                           slope_col, win_mask)
        m_all = jnp.maximum(m, sink_col)
        scale = jnp.exp(m - m_all)
        o = acc * scale / (l * scale + jnp.exp(sink_col - m_all))
        outs += [o[r * Q_BLOCK:(r + 1) * Q_BLOCK] for r in range(rep)]
    o_ref[0] = jnp.concatenate(outs, axis=1).astype(o_ref.dtype)


def _swa_attention(proj, sinks):
    b, s, _ = proj.shape
    nq = s // Q_BLOCK
    return pl.pallas_call(
        _swa_body,
        grid=(b, nq),
        in_specs=[
            pl.BlockSpec(memory_space=pltpu.SMEM),
            pl.BlockSpec((1, Q_BLOCK, SWA_WIDTH), lambda i, j: (i, j, COL_SWA_Q // SWA_WIDTH)),
            pl.BlockSpec((1, s, SWA_KV_WIDTH), lambda i, j: (i, 0, COL_SWA_K // SWA_KV_WIDTH)),
            pl.BlockSpec((1, s, SWA_KV_WIDTH), lambda i, j: (i, 0, COL_SWA_V // SWA_KV_WIDTH)),
        ],
        out_specs=pl.BlockSpec((1, Q_BLOCK, SWA_WIDTH), lambda i, j: (i, j, 0)),
        out_shape=jax.ShapeDtypeStruct((b, s, SWA_WIDTH), BF16),
        compiler_params=_cparams(("parallel", "arbitrary")),
        name="swa_attention",
    )(sinks, proj, proj, proj)


GDN_PAIR = 2 * GDN_CHUNK


def _gdn_body(q_ref, k_ref, v_ref, z_ref, ar_ref, ac_ref, bc_ref, cwq_ref, cwk_ref, cwv_ref, alog_ref, dtb_ref,
              nw_ref, o_ref, qn_s, kn_s, v_s, u_s, w_s, qd_s, kdt_s, at_s, eg_s, gr_s, gc_s, bc_s):
    s = q_ref.shape[1]
    n_pairs = s // GDN_PAIR
    rowi = lax.broadcasted_iota(jnp.int32, (s, 1), 0)

    def conv_silu(x, w):
        y = x * w[GDN_CONV - 1:GDN_CONV, :]
        for j in range(GDN_CONV - 1):
            sh = GDN_CONV - 1 - j
            xs = jnp.where(rowi >= sh, pltpu.roll(x, sh, axis=0), 0.0)
            y = y + xs * w[j:j + 1, :]
        return _silu(y)

    def l2n(t):
        return t * lax.rsqrt(jnp.sum(t * t, axis=-1, keepdims=True) + NORM_EPS)

    qn_s[...] = l2n(conv_silu(q_ref[0], cwq_ref[...])) * ATTN_SCALE
    kn_s[...] = l2n(conv_silu(k_ref[0], cwk_ref[...]))
    v_s[...] = conv_silu(v_ref[0], cwv_ref[...])

    a_rate = jnp.exp(alog_ref[0])
    dtb = dtb_ref[0]

    def log_decay(a):
        x = a + dtb
        return -(a_rate * (jnp.maximum(x, 0.0) + jnp.log(1.0 + jnp.exp(-jnp.abs(x)))))

    gr_s[...] = log_decay(ar_ref[0, 0])
    gc_s[...] = log_decay(ac_ref[0, 0])
    bc_s[...] = jax.nn.sigmoid(bc_ref[0, 0])

    ii = lax.broadcasted_iota(jnp.int32, (GDN_PAIR, GDN_PAIR), 0)
    jj = lax.broadcasted_iota(jnp.int32, (GDN_PAIR, GDN_PAIR), 1)
    same = _div(ii, GDN_CHUNK) == _div(jj, GDN_CHUNK)
    incl = same & (ii >= jj)
    strict = same & (ii > jj)
    incl_t = same & (jj >= ii)
    eye = jnp.where(ii == jj, 1.0, 0.0)
    last_of = same & (_mod(jj, GDN_CHUNK) == GDN_CHUNK - 1)
    n_double = int(np.log2(GDN_CHUNK)) - 1

    def prepare(p, carry):
        r0 = pl.multiple_of(p * GDN_PAIR, GDN_PAIR)
        rows = pl.ds(r0, GDN_PAIR)
        g_row = gr_s[pl.ds(p, 1), :]
        g_col = gc_s[rows, :]
        beta = bc_s[rows, :]
        gc_col = jnp.sum(jnp.where(incl, g_row, 0.0), axis=1, keepdims=True)
        gc_row = jnp.sum(jnp.where(incl_t, g_col, 0.0), axis=0, keepdims=True)
        decay = jnp.exp(jnp.where(incl, gc_col - gc_row, NEG_INF))
        gl_col = jnp.sum(jnp.where(last_of, gc_row, 0.0), axis=1, keepdims=True)
        q = qn_s[rows, :]
        k = kn_s[rows, :]
        v = v_s[rows, :]
        kb = k * beta
        kf = k.astype(BF16)
        lmat = jnp.where(strict, _dot_nt(kb.astype(BF16), kf) * decay, 0.0)
        pw = -lmat
        tinv = eye + pw
        for _ in range(n_double):
            pw = _dot_hi(pw, pw)
            tinv = tinv + _dot_hi(tinv, pw)
        egc = jnp.exp(gc_col)
        sol = _dot_hi(tinv, jnp.concatenate([v * beta, kb * egc], axis=1))
        u_s[rows, :] = sol[:, :HEAD_DIM]
        w_s[rows, :] = sol[:, HEAD_DIM:]
        at_s[rows, :] = jnp.where(incl, _dot_nt(q.astype(BF16), kf) * decay, 0.0)
        qd_s[rows, :] = q * egc
        kdt_s[rows, :] = (k * jnp.exp(gl_col - gc_col)).T
        eg_s[rows, :] = jnp.broadcast_to(jnp.exp(gl_col), (GDN_PAIR, HEAD_DIM))
        return carry

    lax.fori_loop(0, n_pairs, prepare, 0)

    rin = lax.broadcasted_iota(jnp.int32, (GDN_PAIR, 1), 0)
    nw = nw_ref[...]

    def scan(p, state):
        r0 = pl.multiple_of(p * GDN_PAIR, GDN_PAIR)
        rows = pl.ds(r0, GDN_PAIR)
        u = u_s[rows, :]
        w = w_s[rows, :].astype(BF16)
        qd = qd_s[rows, :].astype(BF16)
        at = at_s[rows, :].astype(BF16)
        kdt = kdt_s[rows, :].astype(BF16)
        eg = eg_s[rows, :]
        o = jnp.zeros((GDN_PAIR, HEAD_DIM), F32)
        for c in range(GDN_PAIR // GDN_CHUNK):
            in_c = _div(rin, GDN_CHUNK) == c
            sb = state.astype(BF16)
            v_new = jnp.where(in_c, u - _dot(w, sb), 0.0)
            vb = v_new.astype(BF16)
            o = jnp.where(in_c, _dot(qd, sb) + _dot(at, vb), o)
            state = state * eg[c * GDN_CHUNK:c * GDN_CHUNK + 1, :] + _dot(kdt, vb)
        o = o * lax.rsqrt(jnp.mean(o * o, axis=-1, keepdims=True) + NORM_EPS) * nw
        o_ref[0, rows, :] = (o * _silu(z_ref[0, rows, :])).astype(o_ref.dtype)
        return state

    lax.fori_loop(0, n_pairs, scan, jnp.zeros((HEAD_DIM, HEAD_DIM), F32))


def _gdn_mixer(proj, conv_w, a_log, dt_bias, norm_w):
    b, s, _ = proj.shape
    h = GDN_HEADS
    n_pairs = s // GDN_PAIR
    ga = proj[:, :, COL_SMALL + SMALL_A:COL_SMALL + SMALL_A + h].transpose(0, 2, 1)
    gb = proj[:, :, COL_SMALL + SMALL_B:COL_SMALL + SMALL_B + h].transpose(0, 2, 1)
    base = COL_GDN // HEAD_DIM
    seq = lambda off: pl.BlockSpec((1, s, HEAD_DIM), lambda i, j: (i, 0, base + off * h + j))
    colv = pl.BlockSpec((1, 1, s, 1), lambda i, j: (i, j, 0, 0))
    cw = lambda off: pl.BlockSpec((GDN_CONV, HEAD_DIM), lambda i, j: (0, off * h + j))
    scalar = pl.BlockSpec((1, 1, 1), lambda i, j: (j, 0, 0))
    seq_f32 = pltpu.VMEM((s, HEAD_DIM), F32)
    return pl.pallas_call(
        _gdn_body,
        grid=(b, h),
        in_specs=[
            seq(0), seq(1), seq(2), seq(3),
            pl.BlockSpec((1, 1, n_pairs, GDN_PAIR), lambda i, j: (i, j, 0, 0)),
            colv, colv,
            cw(0), cw(1), cw(2),
            scalar, scalar,
            pl.BlockSpec((1, HEAD_DIM), lambda i, j: (0, 0)),
        ],
        out_specs=pl.BlockSpec((1, s, HEAD_DIM), lambda i, j: (i, 0, j)),
        out_shape=jax.ShapeDtypeStruct((b, s, GDN_WIDTH), BF16),
        scratch_shapes=[seq_f32] * 9 + [pltpu.VMEM((n_pairs, GDN_PAIR), F32), pltpu.VMEM((s, 1), F32),
                                        pltpu.VMEM((s, 1), F32)],
        compiler_params=_cparams(("parallel", "arbitrary")),
        name="gdn_mixer",
    )(proj, proj, proj, proj, ga.reshape(b, h, n_pairs, GDN_PAIR), ga.reshape(b, h, s, 1),
      gb.reshape(b, h, s, 1), conv_w, conv_w, conv_w, a_log.reshape(h, 1, 1), dt_bias.reshape(h, 1, 1),
      norm_w.reshape(1, HEAD_DIM))


def _outproj_body(on_ref, og_ref, os_ref, x_ref, w_ref, o_ref):
    mix = jnp.concatenate([on_ref[...], og_ref[...], os_ref[...]], axis=1)
    o_ref[...] = x_ref[...] + _dot(mix, w_ref[...])


def _pack_bf16_pairs(h):
    c = h.shape[1] // 2
    bits = lax.bitcast_convert_type(h.astype(BF16).astype(F32), jnp.uint32)
    return bits[:, c:] | (bits[:, :c] >> 16)


def _unpack_bf16_pairs(pk):
    lo = lax.bitcast_convert_type(pk << 16, F32)
    hi = lax.bitcast_convert_type(pk & jnp.uint32(0xFFFF0000), F32)
    return jnp.concatenate([lo.astype(BF16), hi.astype(BF16)], axis=1)


def _outproj_router_body(on_ref, og_ref, os_ref, x_ref, w_ref, nw_ref, r_ref, o_ref, h_ref, lg_ref):
    mix = jnp.concatenate([on_ref[...], og_ref[...], os_ref[...]], axis=1)
    x1 = x_ref[...] + _dot(mix, w_ref[...])
    o_ref[...] = x1
    h = _rms(x1, nw_ref[...])
    h_ref[...] = _pack_bf16_pairs(h)
    lg_ref[...] = _dot_hi(h, r_ref[...])


def _out_proj(o_nsa, o_gdn, o_swa, x2d, w_bf16, ffn_norm=None, router=None, tm=512):
    n, d = x2d.shape
    row = lambda wid: pl.BlockSpec((tm, wid), lambda i: (i, 0))
    full = lambda a, c: pl.BlockSpec((a, c), lambda i: (0, 0))
    in_specs = [row(NSA_WIDTH), row(GDN_WIDTH), row(SWA_WIDTH), row(d), full(d, d)]
    args = [o_nsa, o_gdn, o_swa, x2d, w_bf16]
    if router is None:
        return pl.pallas_call(
            _outproj_body, grid=(n // tm,), in_specs=in_specs, out_specs=row(d),
            out_shape=jax.ShapeDtypeStruct((n, d), F32),
            compiler_params=_cparams(("parallel",)), name="out_proj",
        )(*args)
    router_pad = jnp.pad(router, ((0, 0), (0, LANES - router.shape[1])))
    return pl.pallas_call(
        _outproj_router_body, grid=(n // tm,),
        in_specs=in_specs + [full(1, d), full(d, LANES)],
        out_specs=[row(d), row(d // 2), row(LANES)],
        out_shape=[jax.ShapeDtypeStruct((n, d), F32), jax.ShapeDtypeStruct((n, d // 2), jnp.uint32),
                   jax.ShapeDtypeStruct((n, LANES), F32)],
        compiler_params=_cparams(("parallel",)), name="out_proj_router",
    )(*args, ffn_norm.reshape(1, d), router_pad)


def _swiglu_tile(h, wg, wu, wd):
    a = _dot(h, wg.astype(BF16))
    b = _dot(h, wu.astype(BF16))
    return _dot((_silu(a) * b).astype(BF16), wd.astype(BF16))


def _dense_body(x_ref, nw_ref, wg_ref, wu_ref, wd_ref, o_ref, h_ref):
    @pl.when(pl.program_id(1) == 0)
    def _():
        x = x_ref[...]
        h_ref[...] = _rms(x, nw_ref[...]).astype(BF16)
        o_ref[...] = x

    o_ref[...] += _swiglu_tile(h_ref[...], wg_ref[...], wu_ref[...], wd_ref[...])


def _dense_ffn(x2d, norm_w, wg, wu, wd, tm=1024, tf=256):
    n, d = x2d.shape
    ff = wg.shape[1]
    return pl.pallas_call(
        _dense_body,
        grid=(n // tm, ff // tf),
        in_specs=[
            pl.BlockSpec((tm, d), lambda i, f: (i, 0), pipeline_mode=pl.Buffered(1)),
            pl.BlockSpec((1, d), lambda i, f: (0, 0)),
            pl.BlockSpec((d, tf), lambda i, f: (0, f)),
            pl.BlockSpec((d, tf), lambda i, f: (0, f)),
            pl.BlockSpec((tf, d), lambda i, f: (f, 0)),
        ],
        out_specs=pl.BlockSpec((tm, d), lambda i, f: (i, 0)),
        out_shape=jax.ShapeDtypeStruct((n, d), F32),
        scratch_shapes=[pltpu.VMEM((tm, d), BF16)],
        compiler_params=_cparams(("parallel", "arbitrary")),
        name="dense_ffn",
    )(x2d, norm_w.reshape(1, d), wg, wu, wd)


def _row_gather(idx_ref, base, count, src_hbm, dst, sem):
    def body(r, carry):
        t = idx_ref[base + r]
        pltpu.make_async_copy(src_hbm.at[pl.ds(t, 1), :], dst.at[pl.ds(r, 1), :], sem).start()
        return carry

    lax.fori_loop(0, count, body, 0)


def _row_gather_wait(count, src_hbm, dst, sem):
    pltpu.make_async_copy(src_hbm.at[pl.ds(0, count), :], dst, sem).wait()


def _moe_body(be_ref, nu_ref, tok_ref, hp_hbm, g_ref, wg_ref, wu_ref, wd_ref, o_ref, xbuf, x_s, sem):
    i = pl.program_id(0)
    f = pl.program_id(1)
    tm = x_s.shape[0]
    n_used = nu_ref[0]
    used = i < n_used
    slot = i % 2

    @pl.when(f == 0)
    def _():
        o_ref[...] = jnp.zeros_like(o_ref)

        @pl.when(i == 0)
        def _():
            _row_gather(tok_ref, 0, tm, hp_hbm, xbuf.at[0], sem.at[0])

        @pl.when(used)
        def _():
            _row_gather_wait(tm, hp_hbm, xbuf.at[slot], sem.at[slot])
            x_s[...] = _unpack_bf16_pairs(xbuf[slot])

        @pl.when(i + 1 < n_used)
        def _():
            _row_gather(tok_ref, (i + 1) * tm, tm, hp_hbm, xbuf.at[1 - slot], sem.at[1 - slot])

    @pl.when(used)
    def _():
        o_ref[...] += _swiglu_tile(x_s[...], wg_ref[0], wu_ref[0], wd_ref[0])

    @pl.when(used & (f == pl.num_programs(1) - 1))
    def _():
        o_ref[...] = o_ref[...] * g_ref[...]


def _moe_ffn(h_packed, row_tok, row_gate, blk_expert, n_used, wg, wu, wd, tm, tf=256):
    p = row_tok.shape[0]
    half = h_packed.shape[1]
    d = 2 * half
    ff = wg.shape[2]
    nb = p // tm
    nf = ff // tf

    def fsel(i, f, nu):
        return jnp.where(i < nu[0], f, nf - 1)

    grid_spec = pltpu.PrefetchScalarGridSpec(
        num_scalar_prefetch=3,
        grid=(nb, nf),
        in_specs=[
            pl.BlockSpec(memory_space=pl.ANY),
            pl.BlockSpec((tm, 1), lambda i, f, be, nu, tok: (jnp.minimum(i, nu[0] - 1), 0)),
            pl.BlockSpec((1, d, tf), lambda i, f, be, nu, tok: (be[i], 0, fsel(i, f, nu))),
            pl.BlockSpec((1, d, tf), lambda i, f, be, nu, tok: (be[i], 0, fsel(i, f, nu))),
            pl.BlockSpec((1, tf, d), lambda i, f, be, nu, tok: (be[i], fsel(i, f, nu), 0)),
        ],
        out_specs=pl.BlockSpec((tm, d), lambda i, f, be, nu, tok: (i, 0)),
        scratch_shapes=[pltpu.VMEM((2, tm, half), jnp.uint32), pltpu.VMEM((tm, d), BF16),
                        pltpu.SemaphoreType.DMA((2,))],
    )
    return pl.pallas_call(
        _moe_body,
        grid_spec=grid_spec,
        out_shape=jax.ShapeDtypeStruct((p, d), F32),
        compiler_params=_cparams(("arbitrary", "arbitrary")),
        name="moe_ffn",
    )(blk_expert, n_used, row_tok, h_packed, row_gate, wg, wu, wd)


def _moe_layer(h_packed, logits, wg, wu, wd, tm=1024):
    n = h_packed.shape[0]
    top_logit, top_idx = lax.top_k(logits, TOP_K)
    gate = jax.nn.softmax(top_logit, axis=-1)
    a = n * TOP_K
    e_flat = top_idx.reshape(a)
    onehot = (e_flat[:, None] == jnp.arange(N_EXPERTS, dtype=e_flat.dtype)[None]).astype(jnp.int32)
    rank = jnp.sum((jnp.cumsum(onehot, axis=0) - onehot) * onehot, axis=1)
    counts = jnp.sum(onehot, axis=0)
    padded = (counts + tm - 1) // tm * tm
    pad_end = jnp.cumsum(padded)
    pad_start = pad_end - padded
    dest = pad_start[e_flat] + rank
    nb = a // tm + N_EXPERTS
    p = nb * tm
    tok = jnp.arange(a, dtype=jnp.int32) // TOP_K
    row_tok = jnp.zeros((p,), jnp.int32).at[dest].set(tok)
    row_gate = jnp.zeros((p,), F32).at[dest].set(gate.reshape(a))
    n_used = (pad_end[-1] // tm).astype(jnp.int32)
    blk = jnp.minimum(jnp.arange(nb, dtype=jnp.int32), n_used - 1)
    blk_expert = jnp.clip(jnp.searchsorted(pad_end, blk * tm, side="right"), 0, N_EXPERTS - 1).astype(jnp.int32)
    ys = _moe_ffn(h_packed, row_tok, row_gate.reshape(p, 1), blk_expert, n_used.reshape(1), wg, wu, wd, tm)
    return ys, dest.astype(jnp.int32)


def _combine_body(d_ref, x_ref, w_ref, ys_hbm, o_ref, buf, sem, *, final_norm):
    j = pl.program_id(0)
    tt = x_ref.shape[0]
    slot = j % 2

    def start(step, sl):
        for k in range(TOP_K):
            _row_gather(d_ref, (step * TOP_K + k) * tt, tt, ys_hbm, buf.at[sl, k], sem.at[sl])

    @pl.when(j == 0)
    def _():
        start(0, 0)

    for k in range(TOP_K):
        _row_gather_wait(tt, ys_hbm, buf.at[slot, k], sem.at[slot])

    @pl.when(j + 1 < pl.num_programs(0))
    def _():
        start(j + 1, 1 - slot)

    f = buf[slot, 0]
    for k in range(1, TOP_K):
        f = f + buf[slot, k]
    y = x_ref[...] + f
    o_ref[...] = _rms(y, w_ref[...]) if final_norm else y


def _moe_combine(x2d, ys, dest, norm_w, final_norm, tt=256):
    n, d = x2d.shape
    steps = n // tt
    order = dest.reshape(steps, tt, TOP_K).transpose(0, 2, 1).reshape(n * TOP_K)
    grid_spec = pltpu.PrefetchScalarGridSpec(
        num_scalar_prefetch=1,
        grid=(steps,),
        in_specs=[
            pl.BlockSpec((tt, d), lambda j, dr: (j, 0)),
            pl.BlockSpec((1, d), lambda j, dr: (0, 0)),
            pl.BlockSpec(memory_space=pl.ANY),
        ],
        out_specs=pl.BlockSpec((tt, d), lambda j, dr: (j, 0)),
        scratch_shapes=[pltpu.VMEM((2, TOP_K, tt, d), F32), pltpu.SemaphoreType.DMA((2,))],
    )
    return pl.pallas_call(
        functools.partial(_combine_body, final_norm=final_norm),
        grid_spec=grid_spec,
        out_shape=jax.ShapeDtypeStruct((n, d), F32),
        compiler_params=_cparams(("arbitrary",)),
        name="moe_combine",
    )(order, x2d, norm_w.reshape(1, d), ys)


def kernel(x, attn_norm, w_in, cmp_pos_k, cmp_pos_v, cmp_w1_k, cmp_w2_k, cmp_w1_v, cmp_w2_v, gdn_conv_w, gdn_a_log, gdn_dt_bias, gdn_norm_w, swa_sinks, w_out, ffn_norm, dense_w_gate, dense_w_up, dense_w_down, moe_router, moe_w_gate, moe_w_up, moe_w_down, final_norm):
    b, s, d = x.shape
    n = b * s
    depth = w_in.shape[0]
    assert depth % 2 == 0, "the trunk ends on an expert layer, whose combine feeds the final norm"
    x2d = x.reshape(n, d)
    for layer in range(depth):
        proj = _in_proj(x2d, attn_norm[layer], _reorder_w_in(w_in, layer)).reshape(b, s, PROJ_COLS)
        n_strides = s // CMP_STRIDE
        ck = proj[:, :, COL_NSA_KV:COL_NSA_KV + HEAD_DIM].reshape(b, n_strides, CMP_STRIDE * HEAD_DIM)
        cv = proj[:, :, COL_NSA_KV + HEAD_DIM:COL_NSA_KV + 2 * HEAD_DIM].reshape(b, n_strides, CMP_STRIDE * HEAD_DIM)
        kc, vc = _nsa_compress(ck, cv, cmp_pos_k[layer], cmp_pos_v[layer], cmp_w1_k[layer], cmp_w2_k[layer],
                               cmp_w1_v[layer], cmp_w2_v[layer])
        o_nsa = _nsa_attention(proj, kc, vc).reshape(n, NSA_WIDTH)
        o_gdn = _gdn_mixer(proj, gdn_conv_w[layer], gdn_a_log[layer], gdn_dt_bias[layer],
                           gdn_norm_w[layer]).reshape(n, GDN_WIDTH)
        o_swa = _swa_attention(proj, swa_sinks[layer]).reshape(n, SWA_WIDTH)
        w_o = w_out[layer].astype(BF16)
        i = layer // 2
        if layer % 2 == 0:
            x1 = _out_proj(o_nsa, o_gdn, o_swa, x2d, w_o)
            x2d = _dense_ffn(x1, ffn_norm[layer], dense_w_gate[i], dense_w_up[i], dense_w_down[i])
        else:
            x1, h2, logits = _out_proj(o_nsa, o_gdn, o_swa, x2d, w_o, ffn_norm[layer], moe_router[i])
            ys, dest = _moe_layer(h2, logits[:, :N_EXPERTS], moe_w_gate[i], moe_w_up[i], moe_w_down[i])
            x2d = _moe_combine(x1, ys, dest, final_norm, final_norm=layer == depth - 1)
    return x2d.reshape(b, s, d)
```

```python
import functools

import jax
import jax.numpy as jnp
import numpy as np
from jax import lax
from jax.experimental import pallas as pl
from jax.experimental.pallas import tpu as pltpu

F32 = jnp.float32
BF16 = jnp.bfloat16

D_MODEL = 2048
HEAD_DIM = 128
NSA_HEADS = 4
SWA_HEADS = 4
GDN_HEADS = 8
SWA_KV_HEADS = 2
NSA_WIDTH = NSA_HEADS * HEAD_DIM
GDN_WIDTH = GDN_HEADS * HEAD_DIM
SWA_WIDTH = SWA_HEADS * HEAD_DIM
SWA_KV_WIDTH = SWA_KV_HEADS * HEAD_DIM

CMP_BLOCK = 32
CMP_STRIDE = 16
SEL_BLOCK = 64
SEL_TOPN = 8
NSA_WINDOW = 512
FORCE_BONUS = 1000.0
GDN_CONV = 4
GDN_CHUNK = 64
SWA_WINDOW = 128
Q_BLOCK = 128
N_EXPERTS = 8
TOP_K = 2
NORM_EPS = 1e-6
NEG_INF = -1e30
ATTN_SCALE = HEAD_DIM ** -0.5

LANES = 128
VMEM_LIMIT_BYTES = 56 * 1024 * 1024

COL_NSA_Q = 0
COL_SWA_Q = 512
COL_GDN = 1024
COL_NSA_KV = COL_GDN + 4 * GDN_WIDTH
COL_SWA_K = COL_NSA_KV + 6 * HEAD_DIM
COL_SWA_V = COL_SWA_K + SWA_KV_WIDTH
COL_SMALL = COL_SWA_V + SWA_KV_WIDTH
PROJ_COLS = 6656
PROJ_TN = 1664
SMALL_A = 3 * NSA_HEADS
SMALL_B = SMALL_A + GDN_HEADS


def _alibi_slopes():
    n = NSA_HEADS + SWA_HEADS
    s = [2.0 ** (-8.0 * i / n) for i in range(1, n + 1)]
    return tuple(s[SWA_HEADS:]), tuple(s[:SWA_HEADS])


NSA_SLOPES, SWA_SLOPES = _alibi_slopes()


def _div(v, c):
    assert c & (c - 1) == 0
    return v >> (c.bit_length() - 1)


def _mod(v, c):
    assert c & (c - 1) == 0
    return v & (c - 1)


def _cparams(sem):
    return pltpu.CompilerParams(dimension_semantics=sem, vmem_limit_bytes=VMEM_LIMIT_BYTES)


def _rms(x, w):
    return x * lax.rsqrt(jnp.mean(x * x, axis=-1, keepdims=True) + NORM_EPS) * w


def _silu(x):
    return x * jax.nn.sigmoid(x)


def _dot(a, b):
    return jnp.dot(a, b, preferred_element_type=F32)


def _dot_nt(a, b):
    return lax.dot_general(a, b, (((1,), (1,)), ((), ())), preferred_element_type=F32)


def _split2(a):
    hi = a.astype(BF16)
    lo = (a - hi.astype(F32)).astype(BF16)
    return hi, lo


def _dot_hi(a, b):
    ah, al = _split2(a)
    bh, bl = _split2(b)
    return _dot(ah, bh) + (_dot(ah, bl) + _dot(al, bh))


def _dot_exact_rhs(a, b_bf16):
    a0 = a.astype(BF16)
    r = a - a0.astype(F32)
    a1 = r.astype(BF16)
    a2 = (r - a1.astype(F32)).astype(BF16)
    return _dot(a0, b_bf16) + (_dot(a1, b_bf16) + _dot(a2, b_bf16))


def _inproj_body(x_ref, nw_ref, w_ref, o_ref, h_ref):
    @pl.when(pl.program_id(1) == 0)
    def _():
        h_ref[...] = _rms(x_ref[...], nw_ref[...]).astype(BF16)

    o_ref[...] = _dot(h_ref[...], w_ref[...])


def _in_proj(x2d, norm_w, w, tm=1024):
    n, d = x2d.shape
    c = w.shape[1]
    return pl.pallas_call(
        _inproj_body,
        grid=(n // tm, c // PROJ_TN),
        in_specs=[
            pl.BlockSpec((tm, d), lambda i, j: (i, 0), pipeline_mode=pl.Buffered(1)),
            pl.BlockSpec((1, d), lambda i, j: (0, 0)),
            pl.BlockSpec((d, PROJ_TN), lambda i, j: (0, j)),
        ],
        out_specs=pl.BlockSpec((tm, PROJ_TN), lambda i, j: (i, j)),
        out_shape=jax.ShapeDtypeStruct((n, c), F32),
        scratch_shapes=[pltpu.VMEM((tm, d), BF16)],
        compiler_params=_cparams(("parallel", "arbitrary")),
        name="in_proj",
    )(x2d, norm_w.reshape(1, d), w)


def _w_in_segments():
    src = {}
    o = 0
    for name, size in (("nsa_q", NSA_WIDTH), ("nsa_kv", 6 * HEAD_DIM), ("nsa_g", 3 * NSA_HEADS),
                       ("gdn", 4 * GDN_WIDTH), ("ga", GDN_HEADS), ("gb", GDN_HEADS),
                       ("swa_q", SWA_WIDTH), ("swa_k", SWA_KV_WIDTH), ("swa_v", SWA_KV_WIDTH)):
        src[name] = (o, size)
        o += size
    dst = {"nsa_q": COL_NSA_Q, "swa_q": COL_SWA_Q, "gdn": COL_GDN, "nsa_kv": COL_NSA_KV, "swa_k": COL_SWA_K,
           "swa_v": COL_SWA_V, "nsa_g": COL_SMALL, "ga": COL_SMALL + SMALL_A, "gb": COL_SMALL + SMALL_B}
    return [(src[k][0], dst[k], src[k][1]) for k in src], o


def _reorder_body(w_ref, o_ref):
    segs, _ = _w_in_segments()
    tail = COL_SMALL + SMALL_B + GDN_HEADS
    for s0, d0, wid in segs:
        o_ref[:, d0:d0 + wid] = w_ref[0, :, s0:s0 + wid].astype(o_ref.dtype)
    o_ref[:, tail:] = jnp.zeros((o_ref.shape[0], o_ref.shape[1] - tail), o_ref.dtype)


def _reorder_w_in(w_all, layer, tr=256):
    _, d, c = w_all.shape
    assert _w_in_segments()[1] == c
    return pl.pallas_call(
        _reorder_body,
        grid=(d // tr,),
        in_specs=[pl.BlockSpec((1, tr, c), lambda i: (layer, i, 0))],
        out_specs=pl.BlockSpec((tr, PROJ_COLS), lambda i: (i, 0)),
        out_shape=jax.ShapeDtypeStruct((d, PROJ_COLS), BF16),
        compiler_params=_cparams(("parallel",)),
        name="reorder_w_in",
    )(w_all)


KEY_TILE = 256


def _key_tile(ref, col0=0):
    return lambda kt: ref[0, pl.ds(pl.multiple_of(kt * KEY_TILE, KEY_TILE), KEY_TILE), col0:col0 + HEAD_DIM]


def _flash(qs, get_k, get_v, lo, hi, t_col, slope_col, mask_fn):
    r = qs.shape[0]
    lane = lax.broadcasted_iota(jnp.int32, (r, KEY_TILE), 1)

    def body(kt, carry):
        m, l, acc = carry
        k = get_k(kt).astype(BF16)
        v = get_v(kt).astype(BF16)
        s = _dot_nt(qs, k) * ATTN_SCALE
        dist = t_col - (kt * KEY_TILE + lane)
        mask = mask_fn(kt, dist)
        logits = jnp.where(mask, s - slope_col * dist.astype(F32), NEG_INF)
        m_new = jnp.maximum(m, jnp.max(logits, axis=-1, keepdims=True))
        alpha = jnp.exp(m - m_new)
        p = jnp.where(mask, jnp.exp(logits - m_new), 0.0)
        l = alpha * l + jnp.sum(p, axis=-1, keepdims=True)
        acc = alpha * acc + _dot(p.astype(BF16), v)
        return m_new, l, acc

    init = (jnp.full((r, 1), NEG_INF, F32), jnp.zeros((r, 1), F32), jnp.zeros((r, HEAD_DIM), F32))
    return lax.fori_loop(lo, hi, body, init)


def _stack_heads(x, heads):
    return jnp.concatenate([x[:, h * HEAD_DIM:(h + 1) * HEAD_DIM] for h in heads], axis=0)


def _head_const_col(rows, values):
    head = _div(lax.broadcasted_iota(jnp.int32, (rows, 1), 0), Q_BLOCK)
    col = jnp.full((rows, 1), values[-1], F32)
    for h in range(len(values) - 2, -1, -1):
        col = jnp.where(head == h, values[h], col)
    return col


def _compress_body(ck_ref, cv_ref, pk_ref, pv_ref, w1k_ref, w2k_ref, w1v_ref, w2v_ref, kc_ref, vc_ref):
    half = CMP_STRIDE * HEAD_DIM

    def run(c_ref, p_ref, w1_ref, w2_ref, o_ref):
        c = c_ref[0]
        lo = _dot((c + p_ref[0:1, :]).astype(BF16), w1_ref[0:half, :].astype(BF16))
        hi = _dot((c + p_ref[1:2, :]).astype(BF16), w1_ref[half:2 * half, :].astype(BF16))
        hid = lo + pltpu.roll(hi, hi.shape[0] - 1, axis=0)
        o_ref[0] = _dot(_silu(hid).astype(BF16), w2_ref[...].astype(BF16))

    run(ck_ref, pk_ref, w1k_ref, w2k_ref, kc_ref)
    run(cv_ref, pv_ref, w1v_ref, w2v_ref, vc_ref)


def _nsa_compress(ck, cv, pos_k, pos_v, w1k, w2k, w1v, w2v):
    b, ns, wid = ck.shape
    hid = w1k.shape[1]
    full = lambda shape: pl.BlockSpec(shape, lambda i: (0,) * len(shape))
    return pl.pallas_call(
        _compress_body,
        grid=(b,),
        in_specs=[
            pl.BlockSpec((1, ns, wid), lambda i: (i, 0, 0)),
            pl.BlockSpec((1, ns, wid), lambda i: (i, 0, 0)),
            full((2, wid)), full((2, wid)),
            full((2 * wid, hid)), full((hid, HEAD_DIM)),
            full((2 * wid, hid)), full((hid, HEAD_DIM)),
        ],
        out_specs=[pl.BlockSpec((1, ns, HEAD_DIM), lambda i: (i, 0, 0))] * 2,
        out_shape=[jax.ShapeDtypeStruct((b, ns, HEAD_DIM), F32)] * 2,
        compiler_params=_cparams(("parallel",)),
        name="nsa_compress",
    )(ck, cv, pos_k.reshape(2, wid), pos_v.reshape(2, wid), w1k, w2k, w1v, w2v)


def _nsa_body(q_ref, gl_ref, kc_ref, vc_ref, ksl_ref, vsl_ref, kw_ref, vw_ref, o_ref, *, n_sel):
    qi = pl.program_id(1)
    h4 = range(NSA_HEADS)
    rows = NSA_HEADS * Q_BLOCK
    qs = _stack_heads(q_ref[0], h4).astype(BF16)
    row = lax.broadcasted_iota(jnp.int32, (rows, 1), 0)
    t_col = qi * Q_BLOCK + _mod(row, Q_BLOCK)
    slope_col = _head_const_col(rows, NSA_SLOPES)

    lane = lax.broadcasted_iota(jnp.int32, (rows, LANES), 1)
    c_dist = t_col - (lane * CMP_STRIDE + CMP_BLOCK - 1)
    c_mask = c_dist >= 0
    sc = _dot_nt(qs, kc_ref[0].astype(BF16)) * ATTN_SCALE
    logits = jnp.where(c_mask, sc - slope_col * c_dist.astype(F32), NEG_INF)
    e = jnp.exp(logits - jnp.max(logits, axis=-1, keepdims=True))
    p = e / jnp.sum(e, axis=-1, keepdims=True)
    p = p * (t_col >= CMP_BLOCK - 1).astype(F32)
    o_cmp = _dot(p.astype(BF16), vc_ref[0].astype(BF16))

    psum = p[0:Q_BLOCK] + p[Q_BLOCK:2 * Q_BLOCK] + p[2 * Q_BLOCK:3 * Q_BLOCK] + p[3 * Q_BLOCK:4 * Q_BLOCK]
    cn = lax.broadcasted_iota(jnp.int32, (LANES, LANES), 0) * CMP_STRIDE
    sj = lax.broadcasted_iota(jnp.int32, (LANES, LANES), 1) * SEL_BLOCK
    ov = jnp.maximum(jnp.minimum(cn + CMP_BLOCK, sj + SEL_BLOCK) - jnp.maximum(cn, sj), 0)
    ov = (ov.astype(F32) * (1.0 / CMP_BLOCK)).astype(BF16)
    imp = _dot_exact_rhs(psum, ov)
    blk = lax.broadcasted_iota(jnp.int32, (Q_BLOCK, LANES), 1)
    tq = qi * Q_BLOCK + lax.broadcasted_iota(jnp.int32, (Q_BLOCK, 1), 0)
    cur = _div(tq, SEL_BLOCK)
    forced = (blk == 0) | (blk == cur) | (blk == cur - 1)
    score = jnp.where(blk <= cur, imp + jnp.where(forced, FORCE_BONUS, 0.0), -1.0)
    score_t = score.T[0:n_sel, :]
    blk_t = lax.broadcasted_iota(jnp.int32, (n_sel, Q_BLOCK), 0)
    rank = jnp.zeros((n_sel, Q_BLOCK), F32)
    for i in range(n_sel):
        s_i = score_t[i:i + 1, :]
        ahead = (s_i > score_t) | ((s_i == score_t) & (blk_t > i))
        rank = rank + jnp.where(ahead, 1.0, 0.0)
    sel_t = jnp.where(rank < float(min(SEL_TOPN, n_sel)), 1.0, 0.0)
    sel_b = jnp.concatenate([sel_t, jnp.zeros((LANES - n_sel, Q_BLOCK), F32)], axis=0).T.astype(BF16)

    ej = lax.broadcasted_iota(jnp.int32, (LANES, KEY_TILE), 0)
    ec = _div(lax.broadcasted_iota(jnp.int32, (LANES, KEY_TILE), 1), SEL_BLOCK)
    per_tile = KEY_TILE // SEL_BLOCK
    q_per_key_tile = KEY_TILE // Q_BLOCK
    last_tile = qi // q_per_key_tile

    def sel_mask(kt, dist):
        expand = jnp.where(ej == kt * per_tile + ec, 1.0, 0.0).astype(BF16)
        member = _dot(sel_b, expand)
        member = jnp.concatenate([member] * NSA_HEADS, axis=0)
        return (member > 0.5) & (dist >= 0)

    m, l, acc = _flash(qs, _key_tile(ksl_ref), _key_tile(vsl_ref), 0, last_tile + 1, t_col, slope_col, sel_mask)
    o_slc = acc / l

    def win_mask(kt, dist):
        return (dist >= 0) & (dist < NSA_WINDOW)

    n_prev = -(-(NSA_WINDOW - 1) // Q_BLOCK)
    first_tile = jnp.maximum(qi - n_prev, 0) // q_per_key_tile
    m, l, acc = _flash(qs, _key_tile(kw_ref), _key_tile(vw_ref), first_tile, last_tile + 1, t_col, slope_col,
                       win_mask)
    o_win = acc / l

    g = jax.nn.sigmoid(gl_ref[0])

    def gate(branch):
        return jnp.concatenate([g[:, branch * NSA_HEADS + h:branch * NSA_HEADS + h + 1] for h in h4], axis=0)

    o = gate(0) * o_cmp + gate(1) * o_slc + gate(2) * o_win
    o_ref[0] = jnp.concatenate([o[h * Q_BLOCK:(h + 1) * Q_BLOCK] for h in h4], axis=1).astype(o_ref.dtype)


def _nsa_attention(proj, kc, vc):
    b, s, _ = proj.shape
    nq = s // Q_BLOCK
    kvb = COL_NSA_KV // HEAD_DIM
    seq = lambda c: pl.BlockSpec((1, s, HEAD_DIM), lambda i, j: (i, 0, c))
    return pl.pallas_call(
        functools.partial(_nsa_body, n_sel=s // SEL_BLOCK),
        grid=(b, nq),
        in_specs=[
            pl.BlockSpec((1, Q_BLOCK, NSA_WIDTH), lambda i, j: (i, j, COL_NSA_Q // NSA_WIDTH)),
            pl.BlockSpec((1, Q_BLOCK, LANES), lambda i, j: (i, j, COL_SMALL // LANES)),
            pl.BlockSpec((1, kc.shape[1], HEAD_DIM), lambda i, j: (i, 0, 0)),
            pl.BlockSpec((1, vc.shape[1], HEAD_DIM), lambda i, j: (i, 0, 0)),
            seq(kvb + 2), seq(kvb + 3), seq(kvb + 4), seq(kvb + 5),
        ],
        out_specs=pl.BlockSpec((1, Q_BLOCK, NSA_WIDTH), lambda i, j: (i, j, 0)),
        out_shape=jax.ShapeDtypeStruct((b, s, NSA_WIDTH), BF16),
        compiler_params=_cparams(("parallel", "arbitrary")),
        name="nsa_attention",
    )(proj, proj, kc, vc, proj, proj, proj, proj)


def _swa_body(sink_ref, q_ref, k_ref, v_ref, o_ref):
    qi = pl.program_id(1)
    rep = SWA_HEADS // SWA_KV_HEADS
    rows = rep * Q_BLOCK
    row = lax.broadcasted_iota(jnp.int32, (rows, 1), 0)
    t_col = qi * Q_BLOCK + _mod(row, Q_BLOCK)
    n_prev = -(-(SWA_WINDOW - 1) // Q_BLOCK)

    def win_mask(kt, dist):
        return (dist >= 0) & (dist < SWA_WINDOW)

    outs = []
    for g in range(SWA_KV_HEADS):
        heads = [g * rep + r for r in range(rep)]
        qs = _stack_heads(q_ref[0], heads).astype(BF16)
        slope_col = _head_const_col(rows, [SWA_SLOPES[h] for h in heads])
        sink_col = _head_const_col(rows, [sink_ref[h] for h in heads])

        q_per_key_tile = KEY_TILE // Q_BLOCK
        m, l, acc = _flash(qs, _key_tile(k_ref, g * HEAD_DIM), _key_tile(v_ref, g * HEAD_DIM),
                           jnp.maximum(qi - n_prev, 0) // q_per_key_tile, qi // q_per_key_tile + 1, t_col,
                           slope_col, win_mask)
        m_all = jnp.maximum(m, sink_col)
        scale = jnp.exp(m - m_all)
        o = acc * scale / (l * scale + jnp.exp(sink_col - m_all))
        outs += [o[r * Q_BLOCK:(r + 1) * Q_BLOCK] for r in range(rep)]
    o_ref[0] = jnp.concatenate(outs, axis=1).astype(o_ref.dtype)


def _swa_attention(proj, sinks):
    b, s, _ = proj.shape
    nq = s // Q_BLOCK
    return pl.pallas_call(
        _swa_body,
        grid=(b, nq),
        in_specs=[
            pl.BlockSpec(memory_space=pltpu.SMEM),
            pl.BlockSpec((1, Q_BLOCK, SWA_WIDTH), lambda i, j: (i, j, COL_SWA_Q // SWA_WIDTH)),
            pl.BlockSpec((1, s, SWA_KV_WIDTH), lambda i, j: (i, 0, COL_SWA_K // SWA_KV_WIDTH)),
            pl.BlockSpec((1, s, SWA_KV_WIDTH), lambda i, j: (i, 0, COL_SWA_V // SWA_KV_WIDTH)),
        ],
        out_specs=pl.BlockSpec((1, Q_BLOCK, SWA_WIDTH), lambda i, j: (i, j, 0)),
        out_shape=jax.ShapeDtypeStruct((b, s, SWA_WIDTH), BF16),
        compiler_params=_cparams(("parallel", "arbitrary")),
        name="swa_attention",
    )(sinks, proj, proj, proj)


GDN_PAIR = 2 * GDN_CHUNK
GDN_STEP_HEADS = 2
GDN_PREP_PAIRS = 4


def _gdn_body(q_ref, k_ref, v_ref, z_ref, ar_ref, br_ref, cwq_ref, cwk_ref, cwv_ref, alog_ref, dtb_ref,
              nw_ref, o_ref, qn_s, kn_s, v_s, u_s, w_s, qd_s, kdt_s, at_s, eg_s, gr_s, br_s):
    s = q_ref.shape[1]
    n_pairs = s // GDN_PAIR
    heads = range(GDN_STEP_HEADS)
    rowi = lax.broadcasted_iota(jnp.int32, (s, 1), 0)

    def conv_silu(x, w):
        y = x * w[GDN_CONV - 1:GDN_CONV, :]
        for j in range(GDN_CONV - 1):
            sh = GDN_CONV - 1 - j
            xs = jnp.where(rowi >= sh, pltpu.roll(x, sh, axis=0), 0.0)
            y = y + xs * w[j:j + 1, :]
        return _silu(y)

    def l2n(t):
        return t * lax.rsqrt(jnp.sum(t * t, axis=-1, keepdims=True) + NORM_EPS)

    for hh in heads:
        cols = slice(hh * HEAD_DIM, (hh + 1) * HEAD_DIM)
        qn_s[hh] = l2n(conv_silu(q_ref[0, :, cols], cwq_ref[:, cols])) * ATTN_SCALE
        kn_s[hh] = l2n(conv_silu(k_ref[0, :, cols], cwk_ref[:, cols]))
        v_s[hh] = conv_silu(v_ref[0, :, cols], cwv_ref[:, cols])
        a_rate = jnp.exp(alog_ref[hh])
        x = ar_ref[0, hh] + dtb_ref[hh]
        gr_s[hh] = -(a_rate * (jnp.maximum(x, 0.0) + jnp.log(1.0 + jnp.exp(-jnp.abs(x)))))
        br_s[hh] = jax.nn.sigmoid(br_ref[0, hh])

    ii = lax.broadcasted_iota(jnp.int32, (GDN_PAIR, GDN_PAIR), 0)
    jj = lax.broadcasted_iota(jnp.int32, (GDN_PAIR, GDN_PAIR), 1)
    same = _div(ii, GDN_CHUNK) == _div(jj, GDN_CHUNK)
    incl = same & (ii >= jj)
    strict = same & (ii > jj)
    diag = ii == jj
    eye = jnp.where(diag, 1.0, 0.0)
    last_of = same & (_mod(jj, GDN_CHUNK) == GDN_CHUNK - 1)
    n_double = int(np.log2(GDN_CHUNK)) - 1

    def tile_rows(p):
        start = p * GDN_PAIR
        return pl.ds(start if isinstance(start, int) else pl.multiple_of(start, GDN_PAIR), GDN_PAIR)

    def prepare_stages(grp):
        chains = [(hh, grp * GDN_PREP_PAIRS + pp) for pp in range(GDN_PREP_PAIRS) for hh in heads]
        rows = [tile_rows(p) for _, p in chains]
        n = range(len(chains))
        beta, gc_col, gl_col, decay = [], [], [], []
        for hh, p in chains:
            g_row = gr_s[hh, pl.ds(p, 1), :]
            beta.append(jnp.sum(jnp.where(diag, br_s[hh, pl.ds(p, 1), :], 0.0), axis=1, keepdims=True))
            gcc = jnp.sum(jnp.where(incl, g_row, 0.0), axis=1, keepdims=True)
            gcr = jnp.sum(jnp.where(diag, gcc, 0.0), axis=0, keepdims=True)
            gc_col.append(gcc)
            gl_col.append(jnp.sum(jnp.where(last_of, gcr, 0.0), axis=1, keepdims=True))
            decay.append(jnp.exp(jnp.where(incl, gcc - gcr, NEG_INF)))
        q = [qn_s[hh, rows[c], :] for c, (hh, _) in enumerate(chains)]
        k = [kn_s[hh, rows[c], :] for c, (hh, _) in enumerate(chains)]
        v = [v_s[hh, rows[c], :] for c, (hh, _) in enumerate(chains)]
        kb = [k[c] * beta[c] for c in n]
        kf = [k[c].astype(BF16) for c in n]
        kk = [_dot_nt(kb[c].astype(BF16), kf[c]) for c in n]
        qk = [_dot_nt(q[c].astype(BF16), kf[c]) for c in n]
        yield
        pw = [-jnp.where(strict, kk[c] * decay[c], 0.0) for c in n]
        tinv = [eye + pw[c] for c in n]
        for _ in range(n_double):
            pw = [_dot_hi(pw[c], pw[c]) for c in n]
            yield
            tinv = [tinv[c] + _dot_hi(tinv[c], pw[c]) for c in n]
            yield
        egc = [jnp.exp(gc_col[c]) for c in n]
        sol = [_dot_hi(tinv[c], jnp.concatenate([v[c] * beta[c], kb[c] * egc[c]], axis=1)) for c in n]
        yield
        for c, (hh, _) in enumerate(chains):
            u_s[hh, rows[c], :] = sol[c][:, :HEAD_DIM]
            w_s[hh, rows[c], :] = sol[c][:, HEAD_DIM:]
            at_s[hh, rows[c], :] = jnp.where(incl, qk[c] * decay[c], 0.0)
            qd_s[hh, rows[c], :] = q[c] * egc[c]
            kdt_s[hh, rows[c], :] = (k[c] * jnp.exp(gl_col[c] - gc_col[c])).T
            eg_s[hh, rows[c], :] = jnp.broadcast_to(jnp.exp(gl_col[c]), (GDN_PAIR, HEAD_DIM))

    rin = lax.broadcasted_iota(jnp.int32, (GDN_PAIR, 1), 0)
    nw = nw_ref[...]

    def scan_stages(grp, states):
        for pp in range(GDN_PREP_PAIRS):
            rows = tile_rows(grp * GDN_PREP_PAIRS + pp)
            u = [u_s[hh, rows, :] for hh in heads]
            w = [w_s[hh, rows, :].astype(BF16) for hh in heads]
            qd = [qd_s[hh, rows, :].astype(BF16) for hh in heads]
            at = [at_s[hh, rows, :].astype(BF16) for hh in heads]
            kdt = [kdt_s[hh, rows, :].astype(BF16) for hh in heads]
            eg = [eg_s[hh, rows, :] for hh in heads]
            o = [jnp.zeros((GDN_PAIR, HEAD_DIM), F32) for _ in heads]
            for c in range(GDN_PAIR // GDN_CHUNK):
                in_c = _div(rin, GDN_CHUNK) == c
                sb = [states[hh].astype(BF16) for hh in heads]
                ws = [_dot(w[hh], sb[hh]) for hh in heads]
                qs = [_dot(qd[hh], sb[hh]) for hh in heads]
                yield
                vb = [jnp.where(in_c, u[hh] - ws[hh], 0.0).astype(BF16) for hh in heads]
                av = [_dot(at[hh], vb[hh]) for hh in heads]
                kv = [_dot(kdt[hh], vb[hh]) for hh in heads]
                yield
                o = [jnp.where(in_c, qs[hh] + av[hh], o[hh]) for hh in heads]
                for hh in heads:
                    states[hh] = states[hh] * eg[hh][c * GDN_CHUNK:c * GDN_CHUNK + 1, :] + kv[hh]
            for hh in heads:
                cols = slice(hh * HEAD_DIM, (hh + 1) * HEAD_DIM)
                on = o[hh] * lax.rsqrt(jnp.mean(o[hh] * o[hh], axis=-1, keepdims=True) + NORM_EPS) * nw
                o_ref[0, rows, cols] = (on * _silu(z_ref[0, rows, cols])).astype(o_ref.dtype)

    def alternate(*phases):
        live = list(phases)
        while live:
            for ph in list(live):
                if next(ph, live) is live:
                    live.remove(ph)

    n_groups = n_pairs // GDN_PREP_PAIRS
    alternate(prepare_stages(0))

    def group_step(grp, states):
        states = list(states)
        alternate(prepare_stages(grp), scan_stages(grp - 1, states))
        return tuple(states)

    states = lax.fori_loop(1, n_groups, group_step, tuple(jnp.zeros((HEAD_DIM, HEAD_DIM), F32) for _ in heads))
    alternate(scan_stages(n_groups - 1, list(states)))


def _gdn_mixer(proj, conv_w, a_log, dt_bias, norm_w):
    b, s, _ = proj.shape
    h = GDN_HEADS
    nh = GDN_STEP_HEADS
    n_pairs = s // GDN_PAIR
    wid = nh * HEAD_DIM

    def rows_of(col):
        return proj[:, :, col:col + h].transpose(0, 2, 1).reshape(b, h, n_pairs, GDN_PAIR)

    seq = lambda off: pl.BlockSpec((1, s, wid), lambda i, j: (i, 0, (COL_GDN + off * GDN_WIDTH) // wid + j))
    gate = pl.BlockSpec((1, nh, n_pairs, GDN_PAIR), lambda i, j: (i, j, 0, 0))
    cw = lambda off: pl.BlockSpec((GDN_CONV, wid), lambda i, j: (0, off * (h // nh) + j))
    scalar = pl.BlockSpec((nh, 1, 1), lambda i, j: (j, 0, 0))
    seq_f32 = pltpu.VMEM((nh, s, HEAD_DIM), F32)
    row_f32 = pltpu.VMEM((nh, n_pairs, GDN_PAIR), F32)
    return pl.pallas_call(
        _gdn_body,
        grid=(b, h // nh),
        in_specs=[
            seq(0), seq(1), seq(2), seq(3),
            gate, gate,
            cw(0), cw(1), cw(2),
            scalar, scalar,
            pl.BlockSpec((1, HEAD_DIM), lambda i, j: (0, 0)),
        ],
        out_specs=pl.BlockSpec((1, s, wid), lambda i, j: (i, 0, j)),
        out_shape=jax.ShapeDtypeStruct((b, s, GDN_WIDTH), BF16),
        scratch_shapes=[seq_f32] * 9 + [row_f32, row_f32],
        compiler_params=_cparams(("parallel", "arbitrary")),
        name="gdn_mixer",
    )(proj, proj, proj, proj, rows_of(COL_SMALL + SMALL_A), rows_of(COL_SMALL + SMALL_B), conv_w, conv_w, conv_w,
      a_log.reshape(h, 1, 1), dt_bias.reshape(h, 1, 1), norm_w.reshape(1, HEAD_DIM))


def _outproj_body(on_ref, og_ref, os_ref, x_ref, w_ref, o_ref):
    mix = jnp.concatenate([on_ref[...], og_ref[...], os_ref[...]], axis=1)
    o_ref[...] = x_ref[...] + _dot(mix, w_ref[...])


def _outproj_router_body(on_ref, og_ref, os_ref, x_ref, w_ref, nw_ref, r_ref, o_ref, h_ref, lg_ref):
    mix = jnp.concatenate([on_ref[...], og_ref[...], os_ref[...]], axis=1)
    x1 = x_ref[...] + _dot(mix, w_ref[...])
    o_ref[...] = x1
    h = _rms(x1, nw_ref[...])
    h_ref[...] = h
    lg_ref[...] = _dot_hi(h, r_ref[...])


def _out_proj(o_nsa, o_gdn, o_swa, x2d, w_bf16, ffn_norm=None, router=None, tm=512):
    n, d = x2d.shape
    row = lambda wid: pl.BlockSpec((tm, wid), lambda i: (i, 0))
    full = lambda a, c: pl.BlockSpec((a, c), lambda i: (0, 0))
    in_specs = [row(NSA_WIDTH), row(GDN_WIDTH), row(SWA_WIDTH), row(d), full(d, d)]
    args = [o_nsa, o_gdn, o_swa, x2d, w_bf16]
    if router is None:
        return pl.pallas_call(
            _outproj_body, grid=(n // tm,), in_specs=in_specs, out_specs=row(d),
            out_shape=jax.ShapeDtypeStruct((n, d), F32),
            compiler_params=_cparams(("parallel",)), name="out_proj",
        )(*args)
    router_pad = jnp.pad(router, ((0, 0), (0, LANES - router.shape[1])))
    return pl.pallas_call(
        _outproj_router_body, grid=(n // tm,),
        in_specs=in_specs + [full(1, d), full(d, LANES)],
        out_specs=[row(d), row(d), row(LANES)],
        out_shape=[jax.ShapeDtypeStruct((n, d), F32), jax.ShapeDtypeStruct((n, d), F32),
                   jax.ShapeDtypeStruct((n, LANES), F32)],
        compiler_params=_cparams(("parallel",)), name="out_proj_router",
    )(*args, ffn_norm.reshape(1, d), router_pad)


def _swiglu_tile(h, wg, wu, wd):
    a = _dot(h, wg.astype(BF16))
    b = _dot(h, wu.astype(BF16))
    return _dot((_silu(a) * b).astype(BF16), wd.astype(BF16))


def _dense_body(x_ref, nw_ref, wg_ref, wu_ref, wd_ref, o_ref, h_ref):
    @pl.when(pl.program_id(1) == 0)
    def _():
        x = x_ref[...]
        h_ref[...] = _rms(x, nw_ref[...]).astype(BF16)
        o_ref[...] = x

    o_ref[...] += _swiglu_tile(h_ref[...], wg_ref[...], wu_ref[...], wd_ref[...])


def _dense_ffn(x2d, norm_w, wg, wu, wd, tm=1024, tf=256):
    n, d = x2d.shape
    ff = wg.shape[1]
    return pl.pallas_call(
        _dense_body,
        grid=(n // tm, ff // tf),
        in_specs=[
            pl.BlockSpec((tm, d), lambda i, f: (i, 0), pipeline_mode=pl.Buffered(1)),
            pl.BlockSpec((1, d), lambda i, f: (0, 0)),
            pl.BlockSpec((d, tf), lambda i, f: (0, f)),
            pl.BlockSpec((d, tf), lambda i, f: (0, f)),
            pl.BlockSpec((tf, d), lambda i, f: (f, 0)),
        ],
        out_specs=pl.BlockSpec((tm, d), lambda i, f: (i, 0)),
        out_shape=jax.ShapeDtypeStruct((n, d), F32),
        scratch_shapes=[pltpu.VMEM((tm, d), BF16)],
        compiler_params=_cparams(("parallel", "arbitrary")),
        name="dense_ffn",
    )(x2d, norm_w.reshape(1, d), wg, wu, wd)


def _row_gather(idx_ref, base, count, src_hbm, dst, sem):
    def body(r, carry):
        t = idx_ref[base + r]
        pltpu.make_async_copy(src_hbm.at[pl.ds(t, 1), :], dst.at[pl.ds(r, 1), :], sem).start()
        return carry

    lax.fori_loop(0, count, body, 0)


def _row_gather_wait(count, src_hbm, dst, sem):
    pltpu.make_async_copy(src_hbm.at[pl.ds(0, count), :], dst, sem).wait()


MOE_STEP_ROWS = 32


def _moe_body(be_ref, nu_ref, br_ref, tok_ref, h_hbm, wg_ref, wu_ref, wd_ref, o_ref, xbuf, x_s, sem, *, nf):
    i = pl.program_id(0)
    f = pl.program_id(1)
    tm = x_s.shape[0]
    n_used = nu_ref[0]
    used = i < n_used
    more = i + 1 < n_used
    small = br_ref[i] <= tm // 2
    head_rows = tm - MOE_STEP_ROWS * nf
    assert head_rows >= 0

    @pl.when(f == 0)
    def _():
        o_ref[...] = jnp.zeros_like(o_ref)

        @pl.when(i == 0)
        def _():
            _row_gather(tok_ref, 0, tm, h_hbm, xbuf, sem)

        @pl.when(used)
        def _():
            _row_gather_wait(tm, h_hbm, xbuf, sem)
            x_s[...] = xbuf[...].astype(BF16)

        @pl.when(more)
        def _():
            _row_gather(tok_ref, (i + 1) * tm, head_rows, h_hbm, xbuf, sem)

    def accumulate(rows):
        o_ref[0:rows, :] += _swiglu_tile(x_s[0:rows, :], wg_ref[0], wu_ref[0], wd_ref[0])

    def fetch_next_rows():
        base = head_rows + f * MOE_STEP_ROWS
        for u in range(MOE_STEP_ROWS):
            t = tok_ref[(i + 1) * tm + base + u]
            pltpu.make_async_copy(h_hbm.at[pl.ds(t, 1), :], xbuf.at[pl.ds(base + u, 1), :], sem).start()

    for rows, fits in ((tm // 2, small), (tm, jnp.logical_not(small))):
        @pl.when(more & fits)
        def _(rows=rows):
            fetch_next_rows()
            accumulate(rows)

        @pl.when(used & jnp.logical_not(more) & fits)
        def _(rows=rows):
            accumulate(rows)


def _moe_ffn(h, row_tok, blk_expert, n_used, blk_rows, wg, wu, wd, tm, tf=256):
    p = row_tok.shape[0]
    d = h.shape[1]
    ff = wg.shape[2]
    nb = p // tm
    nf = ff // tf

    def fsel(i, f, nu):
        return jnp.where(i < nu[0], f, nf - 1)

    grid_spec = pltpu.PrefetchScalarGridSpec(
        num_scalar_prefetch=4,
        grid=(nb, nf),
        in_specs=[
            pl.BlockSpec(memory_space=pl.ANY),
            pl.BlockSpec((1, d, tf), lambda i, f, be, nu, br, tok: (be[i], 0, fsel(i, f, nu))),
            pl.BlockSpec((1, d, tf), lambda i, f, be, nu, br, tok: (be[i], 0, fsel(i, f, nu))),
            pl.BlockSpec((1, tf, d), lambda i, f, be, nu, br, tok: (be[i], fsel(i, f, nu), 0)),
        ],
        out_specs=pl.BlockSpec((tm, d), lambda i, f, be, nu, br, tok: (i, 0)),
        scratch_shapes=[pltpu.VMEM((tm, d), F32), pltpu.VMEM((tm, d), BF16), pltpu.SemaphoreType.DMA(())],
    )
    return pl.pallas_call(
        functools.partial(_moe_body, nf=nf),
        grid_spec=grid_spec,
        out_shape=jax.ShapeDtypeStruct((p, d), F32),
        compiler_params=_cparams(("arbitrary", "arbitrary")),
        name="moe_ffn",
    )(blk_expert, n_used, blk_rows, row_tok, h, wg, wu, wd)


def _moe_layer(h, logits, wg, wu, wd, tm=1024):
    n = h.shape[0]
    top_logit, top_idx = lax.top_k(logits, TOP_K)
    gate = jax.nn.softmax(top_logit, axis=-1)
    a = n * TOP_K
    e_flat = top_idx.reshape(a)
    onehot = (e_flat[:, None] == jnp.arange(N_EXPERTS, dtype=e_flat.dtype)[None]).astype(jnp.int32)
    rank = jnp.sum((jnp.cumsum(onehot, axis=0) - onehot) * onehot, axis=1)
    counts = jnp.sum(onehot, axis=0)
    padded = (counts + tm - 1) // tm * tm
    pad_end = jnp.cumsum(padded)
    pad_start = pad_end - padded
    dest = (pad_start[e_flat] + rank).astype(jnp.int32)
    nb = a // tm + N_EXPERTS
    n_used = (pad_end[-1] // tm).astype(jnp.int32)
    blk = jnp.minimum(jnp.arange(nb, dtype=jnp.int32), n_used - 1)
    blk_expert = jnp.sum((blk[:, None] * tm >= pad_end[None, :]).astype(jnp.int32), axis=1)
    blk_expert = jnp.minimum(blk_expert, N_EXPERTS - 1)
    blk_rows = jnp.clip(counts[blk_expert] - (blk * tm - pad_start[blk_expert]), 0, tm).astype(jnp.int32)
    row_tok = jnp.zeros((nb * tm,), jnp.int32).at[dest].set(jnp.arange(a, dtype=jnp.int32) // TOP_K)
    ys = _moe_ffn(h, row_tok, blk_expert, n_used.reshape(1), blk_rows, wg, wu, wd, tm)
    return ys, dest, gate


def _combine_body(d_ref, x_ref, g_ref, w_ref, ys_hbm, o_ref, buf, sem, *, final_norm):
    j = pl.program_id(0)
    tt = x_ref.shape[0]
    slot = j % 2

    def start(step, sl):
        for k in range(TOP_K):
            _row_gather(d_ref, (step * TOP_K + k) * tt, tt, ys_hbm, buf.at[sl, k], sem.at[sl])

    @pl.when(j == 0)
    def _():
        start(0, 0)

    for k in range(TOP_K):
        _row_gather_wait(tt, ys_hbm, buf.at[slot, k], sem.at[slot])

    @pl.when(j + 1 < pl.num_programs(0))
    def _():
        start(j + 1, 1 - slot)

    g = g_ref[...]
    f = buf[slot, 0] * g[:, 0:1]
    for k in range(1, TOP_K):
        f = f + buf[slot, k] * g[:, k:k + 1]
    y = x_ref[...] + f
    o_ref[...] = _rms(y, w_ref[...]) if final_norm else y


def _moe_combine(x2d, ys, dest, gate, norm_w, final_norm, tt=256):
    n, d = x2d.shape
    steps = n // tt
    order = dest.reshape(steps, tt, TOP_K).transpose(0, 2, 1).reshape(n * TOP_K)
    grid_spec = pltpu.PrefetchScalarGridSpec(
        num_scalar_prefetch=1,
        grid=(steps,),
        in_specs=[
            pl.BlockSpec((tt, d), lambda j, dr: (j, 0)),
            pl.BlockSpec((tt, TOP_K), lambda j, dr: (j, 0)),
            pl.BlockSpec((1, d), lambda j, dr: (0, 0)),
            pl.BlockSpec(memory_space=pl.ANY),
        ],
        out_specs=pl.BlockSpec((tt, d), lambda j, dr: (j, 0)),
        scratch_shapes=[pltpu.VMEM((2, TOP_K, tt, d), F32), pltpu.SemaphoreType.DMA((2,))],
    )
    return pl.pallas_call(
        functools.partial(_combine_body, final_norm=final_norm),
        grid_spec=grid_spec,
        out_shape=jax.ShapeDtypeStruct((n, d), F32),
        compiler_params=_cparams(("arbitrary",)),
        name="moe_combine",
    )(order, x2d, gate, norm_w.reshape(1, d), ys)


def kernel(x, attn_norm, w_in, cmp_pos_k, cmp_pos_v, cmp_w1_k, cmp_w2_k, cmp_w1_v, cmp_w2_v, gdn_conv_w, gdn_a_log, gdn_dt_bias, gdn_norm_w, swa_sinks, w_out, ffn_norm, dense_w_gate, dense_w_up, dense_w_down, moe_router, moe_w_gate, moe_w_up, moe_w_down, final_norm):
    b, s, d = x.shape
    n = b * s
    depth = w_in.shape[0]
    assert depth % 2 == 0, "the trunk ends on an expert layer, whose combine feeds the final norm"
    x2d = x.reshape(n, d)
    for layer in range(depth):
        proj = _in_proj(x2d, attn_norm[layer], _reorder_w_in(w_in, layer)).reshape(b, s, PROJ_COLS)
        n_strides = s // CMP_STRIDE
        ck = proj[:, :, COL_NSA_KV:COL_NSA_KV + HEAD_DIM].reshape(b, n_strides, CMP_STRIDE * HEAD_DIM)
        cv = proj[:, :, COL_NSA_KV + HEAD_DIM:COL_NSA_KV + 2 * HEAD_DIM].reshape(b, n_strides, CMP_STRIDE * HEAD_DIM)
        kc, vc = _nsa_compress(ck, cv, cmp_pos_k[layer], cmp_pos_v[layer], cmp_w1_k[layer], cmp_w2_k[layer],
                               cmp_w1_v[layer], cmp_w2_v[layer])
        o_nsa = _nsa_attention(proj, kc, vc).reshape(n, NSA_WIDTH)
        o_gdn = _gdn_mixer(proj, gdn_conv_w[layer], gdn_a_log[layer], gdn_dt_bias[layer],
                           gdn_norm_w[layer]).reshape(n, GDN_WIDTH)
        o_swa = _swa_attention(proj, swa_sinks[layer]).reshape(n, SWA_WIDTH)
        w_o = w_out[layer].astype(BF16)
        i = layer // 2
        if layer % 2 == 0:
            x1 = _out_proj(o_nsa, o_gdn, o_swa, x2d, w_o)
            x2d = _dense_ffn(x1, ffn_norm[layer], dense_w_gate[i], dense_w_up[i], dense_w_down[i])
        else:
            x1, h2, logits = _out_proj(o_nsa, o_gdn, o_swa, x2d, w_o, ffn_norm[layer], moe_router[i])
            ys, dest, gate = _moe_layer(h2, logits[:, :N_EXPERTS], moe_w_gate[i], moe_w_up[i], moe_w_down[i])
            x2d = _moe_combine(x1, ys, dest, gate, final_norm, final_norm=layer == depth - 1)
    return x2d.reshape(b, s, d)
```

```python
import functools

import jax
import jax.numpy as jnp
import numpy as np
from jax import lax
from jax.experimental import pallas as pl
from jax.experimental.pallas import tpu as pltpu

F32 = jnp.float32
BF16 = jnp.bfloat16

D_MODEL = 2048
HEAD_DIM = 128
NSA_HEADS = 4
SWA_HEADS = 4
GDN_HEADS = 8
SWA_KV_HEADS = 2
NSA_WIDTH = NSA_HEADS * HEAD_DIM
GDN_WIDTH = GDN_HEADS * HEAD_DIM
SWA_WIDTH = SWA_HEADS * HEAD_DIM
SWA_KV_WIDTH = SWA_KV_HEADS * HEAD_DIM

CMP_BLOCK = 32
CMP_STRIDE = 16
SEL_BLOCK = 64
SEL_TOPN = 8
NSA_WINDOW = 512
FORCE_BONUS = 1000.0
GDN_CONV = 4
GDN_CHUNK = 64
SWA_WINDOW = 128
Q_BLOCK = 128
N_EXPERTS = 8
TOP_K = 2
NORM_EPS = 1e-6
NEG_INF = -1e30
ATTN_SCALE = HEAD_DIM ** -0.5

LANES = 128
VMEM_LIMIT_BYTES = 56 * 1024 * 1024
W_SPLIT = 2

COL_NSA_Q = 0
COL_SWA_Q = 512
COL_GDN = 1024
COL_NSA_KV = COL_GDN + 4 * GDN_WIDTH
COL_SWA_K = COL_NSA_KV + 6 * HEAD_DIM
COL_SWA_V = COL_SWA_K + SWA_KV_WIDTH
COL_SMALL = COL_SWA_V + SWA_KV_WIDTH
PROJ_COLS = 6656
PROJ_TN = 1664
SMALL_A = 3 * NSA_HEADS
SMALL_B = SMALL_A + GDN_HEADS


def _alibi_slopes():
    n = NSA_HEADS + SWA_HEADS
    s = [2.0 ** (-8.0 * i / n) for i in range(1, n + 1)]
    return tuple(s[SWA_HEADS:]), tuple(s[:SWA_HEADS])


NSA_SLOPES, SWA_SLOPES = _alibi_slopes()


def _div(v, c):
    assert c & (c - 1) == 0
    return v >> (c.bit_length() - 1)


def _mod(v, c):
    assert c & (c - 1) == 0
    return v & (c - 1)


def _cparams(sem):
    return pltpu.CompilerParams(dimension_semantics=sem, vmem_limit_bytes=VMEM_LIMIT_BYTES)


def _rms(x, w):
    return x * lax.rsqrt(jnp.mean(x * x, axis=-1, keepdims=True) + NORM_EPS) * w


def _silu(x):
    return x * jax.nn.sigmoid(x)


def _dot(a, b):
    return jnp.dot(a, b, preferred_element_type=F32)


def _dot_nt(a, b):
    return lax.dot_general(a, b, (((1,), (1,)), ((), ())), preferred_element_type=F32)


def _split2(a):
    hi = a.astype(BF16)
    lo = (a - hi.astype(F32)).astype(BF16)
    return hi, lo


def _dot_hi(a, b):
    ah, al = _split2(a)
    bh, bl = _split2(b)
    return _dot(ah, bh) + (_dot(ah, bl) + _dot(al, bh))


def _dot_exact_rhs(a, b_bf16):
    a0 = a.astype(BF16)
    r = a - a0.astype(F32)
    a1 = r.astype(BF16)
    a2 = (r - a1.astype(F32)).astype(BF16)
    return _dot(a0, b_bf16) + (_dot(a1, b_bf16) + _dot(a2, b_bf16))


def _inproj_body(x_ref, nw_ref, *refs):
    w_refs, (o_ref, h_ref) = refs[:-2], refs[-2:]

    @pl.when(pl.program_id(1) == 0)
    def _():
        h_ref[...] = _rms(x_ref[...], nw_ref[...]).astype(BF16)

    dk = h_ref.shape[1] // len(w_refs)
    o_ref[...] = sum(_dot(h_ref[:, k * dk:(k + 1) * dk], w[...]) for k, w in enumerate(w_refs))


def _in_proj(x2d, norm_w, w, tm=1024):
    n, d = x2d.shape
    c = w.shape[1]
    return pl.pallas_call(
        _inproj_body,
        grid=(n // tm, c // PROJ_TN),
        in_specs=[
            pl.BlockSpec((tm, d), lambda i, j: (i, 0), pipeline_mode=pl.Buffered(1)),
            pl.BlockSpec((1, d), lambda i, j: (0, 0)),
        ] + [pl.BlockSpec((d // W_SPLIT, PROJ_TN), lambda i, j, k=k: (k, j)) for k in range(W_SPLIT)],
        out_specs=pl.BlockSpec((tm, PROJ_TN), lambda i, j: (i, j)),
        out_shape=jax.ShapeDtypeStruct((n, c), F32),
        scratch_shapes=[pltpu.VMEM((tm, d), BF16)],
        compiler_params=_cparams(("parallel", "arbitrary")),
        name="in_proj",
    )(x2d, norm_w.reshape(1, d), *([w] * W_SPLIT))


def _w_in_segments():
    src = {}
    o = 0
    for name, size in (("nsa_q", NSA_WIDTH), ("nsa_kv", 6 * HEAD_DIM), ("nsa_g", 3 * NSA_HEADS),
                       ("gdn", 4 * GDN_WIDTH), ("ga", GDN_HEADS), ("gb", GDN_HEADS),
                       ("swa_q", SWA_WIDTH), ("swa_k", SWA_KV_WIDTH), ("swa_v", SWA_KV_WIDTH)):
        src[name] = (o, size)
        o += size
    dst = {"nsa_q": COL_NSA_Q, "swa_q": COL_SWA_Q, "gdn": COL_GDN, "nsa_kv": COL_NSA_KV, "swa_k": COL_SWA_K,
           "swa_v": COL_SWA_V, "nsa_g": COL_SMALL, "ga": COL_SMALL + SMALL_A, "gb": COL_SMALL + SMALL_B}
    return [(src[k][0], dst[k], src[k][1]) for k in src], o


def _reorder_body(w_ref, o_ref):
    segs, _ = _w_in_segments()
    tail = COL_SMALL + SMALL_B + GDN_HEADS
    for s0, d0, wid in segs:
        o_ref[:, d0:d0 + wid] = w_ref[0, :, s0:s0 + wid].astype(o_ref.dtype)
    o_ref[:, tail:] = jnp.zeros((o_ref.shape[0], o_ref.shape[1] - tail), o_ref.dtype)


def _reorder_w_in(w_all, layer, tr=256):
    _, d, c = w_all.shape
    assert _w_in_segments()[1] == c
    return pl.pallas_call(
        _reorder_body,
        grid=(d // tr,),
        in_specs=[pl.BlockSpec((1, tr, c), lambda i: (layer, i, 0))],
        out_specs=pl.BlockSpec((tr, PROJ_COLS), lambda i: (i, 0)),
        out_shape=jax.ShapeDtypeStruct((d, PROJ_COLS), BF16),
        compiler_params=_cparams(("parallel",)),
        name="reorder_w_in",
    )(w_all)


KEY_TILE = 256


def _key_tile(ref, col0=0):
    return lambda kt: ref[0, pl.ds(pl.multiple_of(kt * KEY_TILE, KEY_TILE), KEY_TILE), col0:col0 + HEAD_DIM]


def _flash(qs, get_k, get_v, lo, hi, t_col, slope_col, mask_fn):
    r = qs.shape[0]
    lane = lax.broadcasted_iota(jnp.int32, (r, KEY_TILE), 1)

    def body(kt, carry):
        m, l, acc = carry
        k = get_k(kt).astype(BF16)
        v = get_v(kt).astype(BF16)
        s = _dot_nt(qs, k) * ATTN_SCALE
        dist = t_col - (kt * KEY_TILE + lane)
        mask = mask_fn(kt, dist)
        logits = jnp.where(mask, s - slope_col * dist.astype(F32), NEG_INF)
        m_new = jnp.maximum(m, jnp.max(logits, axis=-1, keepdims=True))
        alpha = jnp.exp(m - m_new)
        p = jnp.where(mask, jnp.exp(logits - m_new), 0.0)
        l = alpha * l + jnp.sum(p, axis=-1, keepdims=True)
        acc = alpha * acc + _dot(p.astype(BF16), v)
        return m_new, l, acc

    init = (jnp.full((r, 1), NEG_INF, F32), jnp.zeros((r, 1), F32), jnp.zeros((r, HEAD_DIM), F32))
    return lax.fori_loop(lo, hi, body, init)


def _stack_heads(x, heads):
    return jnp.concatenate([x[:, h * HEAD_DIM:(h + 1) * HEAD_DIM] for h in heads], axis=0)


def _head_const_col(rows, values):
    head = _div(lax.broadcasted_iota(jnp.int32, (rows, 1), 0), Q_BLOCK)
    col = jnp.full((rows, 1), values[-1], F32)
    for h in range(len(values) - 2, -1, -1):
        col = jnp.where(head == h, values[h], col)
    return col


def _compress_body(ck_ref, cv_ref, pk_ref, pv_ref, w1k_ref, w2k_ref, w1v_ref, w2v_ref, kc_ref, vc_ref):
    half = CMP_STRIDE * HEAD_DIM

    def run(c_ref, p_ref, w1_ref, w2_ref, o_ref):
        c = c_ref[0]
        lo = _dot((c + p_ref[0:1, :]).astype(BF16), w1_ref[0:half, :].astype(BF16))
        hi = _dot((c + p_ref[1:2, :]).astype(BF16), w1_ref[half:2 * half, :].astype(BF16))
        hid = lo + pltpu.roll(hi, hi.shape[0] - 1, axis=0)
        o_ref[0] = _dot(_silu(hid).astype(BF16), w2_ref[...].astype(BF16))

    run(ck_ref, pk_ref, w1k_ref, w2k_ref, kc_ref)
    run(cv_ref, pv_ref, w1v_ref, w2v_ref, vc_ref)


def _nsa_compress(ck, cv, pos_k, pos_v, w1k, w2k, w1v, w2v):
    b, ns, wid = ck.shape
    hid = w1k.shape[1]
    full = lambda shape: pl.BlockSpec(shape, lambda i: (0,) * len(shape))
    return pl.pallas_call(
        _compress_body,
        grid=(b,),
        in_specs=[
            pl.BlockSpec((1, ns, wid), lambda i: (i, 0, 0)),
            pl.BlockSpec((1, ns, wid), lambda i: (i, 0, 0)),
            full((2, wid)), full((2, wid)),
            full((2 * wid, hid)), full((hid, HEAD_DIM)),
            full((2 * wid, hid)), full((hid, HEAD_DIM)),
        ],
        out_specs=[pl.BlockSpec((1, ns, HEAD_DIM), lambda i: (i, 0, 0))] * 2,
        out_shape=[jax.ShapeDtypeStruct((b, ns, HEAD_DIM), F32)] * 2,
        compiler_params=_cparams(("parallel",)),
        name="nsa_compress",
    )(ck, cv, pos_k.reshape(2, wid), pos_v.reshape(2, wid), w1k, w2k, w1v, w2v)


def _nsa_body(q_ref, gl_ref, kc_ref, vc_ref, ksl_ref, vsl_ref, kw_ref, vw_ref, o_ref, *, n_sel):
    qi = pl.program_id(1)
    h4 = range(NSA_HEADS)
    rows = NSA_HEADS * Q_BLOCK
    qs = _stack_heads(q_ref[0], h4).astype(BF16)
    row = lax.broadcasted_iota(jnp.int32, (rows, 1), 0)
    t_col = qi * Q_BLOCK + _mod(row, Q_BLOCK)
    slope_col = _head_const_col(rows, NSA_SLOPES)

    lane = lax.broadcasted_iota(jnp.int32, (rows, LANES), 1)
    c_dist = t_col - (lane * CMP_STRIDE + CMP_BLOCK - 1)
    c_mask = c_dist >= 0
    sc = _dot_nt(qs, kc_ref[0].astype(BF16)) * ATTN_SCALE
    logits = jnp.where(c_mask, sc - slope_col * c_dist.astype(F32), NEG_INF)
    e = jnp.exp(logits - jnp.max(logits, axis=-1, keepdims=True))
    p = e / jnp.sum(e, axis=-1, keepdims=True)
    p = p * (t_col >= CMP_BLOCK - 1).astype(F32)
    o_cmp = _dot(p.astype(BF16), vc_ref[0].astype(BF16))

    psum = p[0:Q_BLOCK] + p[Q_BLOCK:2 * Q_BLOCK] + p[2 * Q_BLOCK:3 * Q_BLOCK] + p[3 * Q_BLOCK:4 * Q_BLOCK]
    cn = lax.broadcasted_iota(jnp.int32, (LANES, LANES), 0) * CMP_STRIDE
    sj = lax.broadcasted_iota(jnp.int32, (LANES, LANES), 1) * SEL_BLOCK
    ov = jnp.maximum(jnp.minimum(cn + CMP_BLOCK, sj + SEL_BLOCK) - jnp.maximum(cn, sj), 0)
    ov = (ov.astype(F32) * (1.0 / CMP_BLOCK)).astype(BF16)
    imp = _dot_exact_rhs(psum, ov)
    blk = lax.broadcasted_iota(jnp.int32, (Q_BLOCK, LANES), 1)
    tq = qi * Q_BLOCK + lax.broadcasted_iota(jnp.int32, (Q_BLOCK, 1), 0)
    cur = _div(tq, SEL_BLOCK)
    forced = (blk == 0) | (blk == cur) | (blk == cur - 1)
    score = jnp.where(blk <= cur, imp + jnp.where(forced, FORCE_BONUS, 0.0), -1.0)
    score_t = score.T[0:n_sel, :]
    blk_t = lax.broadcasted_iota(jnp.int32, (n_sel, Q_BLOCK), 0)
    rank = jnp.zeros((n_sel, Q_BLOCK), F32)
    for i in range(n_sel):
        s_i = score_t[i:i + 1, :]
        ahead = (s_i > score_t) | ((s_i == score_t) & (blk_t > i))
        rank = rank + jnp.where(ahead, 1.0, 0.0)
    sel_t = jnp.where(rank < float(min(SEL_TOPN, n_sel)), 1.0, 0.0)
    sel_b = jnp.concatenate([sel_t, jnp.zeros((LANES - n_sel, Q_BLOCK), F32)], axis=0).T.astype(BF16)

    ej = lax.broadcasted_iota(jnp.int32, (LANES, KEY_TILE), 0)
    ec = _div(lax.broadcasted_iota(jnp.int32, (LANES, KEY_TILE), 1), SEL_BLOCK)
    per_tile = KEY_TILE // SEL_BLOCK
    q_per_key_tile = KEY_TILE // Q_BLOCK
    last_tile = qi // q_per_key_tile

    def sel_mask(kt, dist):
        expand = jnp.where(ej == kt * per_tile + ec, 1.0, 0.0).astype(BF16)
        member = _dot(sel_b, expand)
        member = jnp.concatenate([member] * NSA_HEADS, axis=0)
        return (member > 0.5) & (dist >= 0)

    m, l, acc = _flash(qs, _key_tile(ksl_ref), _key_tile(vsl_ref), 0, last_tile + 1, t_col, slope_col, sel_mask)
    o_slc = acc / l

    def win_mask(kt, dist):
        return (dist >= 0) & (dist < NSA_WINDOW)

    n_prev = -(-(NSA_WINDOW - 1) // Q_BLOCK)
    first_tile = jnp.maximum(qi - n_prev, 0) // q_per_key_tile
    m, l, acc = _flash(qs, _key_tile(kw_ref), _key_tile(vw_ref), first_tile, last_tile + 1, t_col, slope_col,
                       win_mask)
    o_win = acc / l

    g = jax.nn.sigmoid(gl_ref[0])

    def gate(branch):
        return jnp.concatenate([g[:, branch * NSA_HEADS + h:branch * NSA_HEADS + h + 1] for h in h4], axis=0)

    o = gate(0) * o_cmp + gate(1) * o_slc + gate(2) * o_win
    o_ref[0] = jnp.concatenate([o[h * Q_BLOCK:(h + 1) * Q_BLOCK] for h in h4], axis=1).astype(o_ref.dtype)


def _nsa_attention(proj, kc, vc):
    b, s, _ = proj.shape
    nq = s // Q_BLOCK
    kvb = COL_NSA_KV // HEAD_DIM
    seq = lambda c: pl.BlockSpec((1, s, HEAD_DIM), lambda i, j: (i, 0, c))
    return pl.pallas_call(
        functools.partial(_nsa_body, n_sel=s // SEL_BLOCK),
        grid=(b, nq),
        in_specs=[
            pl.BlockSpec((1, Q_BLOCK, NSA_WIDTH), lambda i, j: (i, j, COL_NSA_Q // NSA_WIDTH)),
            pl.BlockSpec((1, Q_BLOCK, LANES), lambda i, j: (i, j, COL_SMALL // LANES)),
            pl.BlockSpec((1, kc.shape[1], HEAD_DIM), lambda i, j: (i, 0, 0)),
            pl.BlockSpec((1, vc.shape[1], HEAD_DIM), lambda i, j: (i, 0, 0)),
            seq(kvb + 2), seq(kvb + 3), seq(kvb + 4), seq(kvb + 5),
        ],
        out_specs=pl.BlockSpec((1, Q_BLOCK, NSA_WIDTH), lambda i, j: (i, j, 0)),
        out_shape=jax.ShapeDtypeStruct((b, s, NSA_WIDTH), BF16),
        compiler_params=_cparams(("parallel", "arbitrary")),
        name="nsa_attention",
    )(proj, proj, kc, vc, proj, proj, proj, proj)


def _swa_body(sink_ref, q_ref, k_ref, v_ref, o_ref):
    qi = pl.program_id(1)
    rep = SWA_HEADS // SWA_KV_HEADS
    rows = rep * Q_BLOCK
    row = lax.broadcasted_iota(jnp.int32, (rows, 1), 0)
    t_col = qi * Q_BLOCK + _mod(row, Q_BLOCK)
    n_prev = -(-(SWA_WINDOW - 1) // Q_BLOCK)

    def win_mask(kt, dist):
        return (dist >= 0) & (dist < SWA_WINDOW)

    outs = []
    for g in range(SWA_KV_HEADS):
        heads = [g * rep + r for r in range(rep)]
        qs = _stack_heads(q_ref[0], heads).astype(BF16)
        slope_col = _head_const_col(rows, [SWA_SLOPES[h] for h in heads])
        sink_col = _head_const_col(rows, [sink_ref[h] for h in heads])

        q_per_key_tile = KEY_TILE // Q_BLOCK
        m, l, acc = _flash(qs, _key_tile(k_ref, g * HEAD_DIM), _key_tile(v_ref, g * HEAD_DIM),
                           jnp.maximum(qi - n_prev, 0) // q_per_key_tile, qi // q_per_key_tile + 1, t_col,
                           slope_col, win_mask)
        m_all = jnp.maximum(m, sink_col)
        scale = jnp.exp(m - m_all)
        o = acc * scale / (l * scale + jnp.exp(sink_col - m_all))
        outs += [o[r * Q_BLOCK:(r + 1) * Q_BLOCK] for r in range(rep)]
    o_ref[0] = jnp.concatenate(outs, axis=1).astype(o_ref.dtype)


def _swa_attention(proj, sinks):
    b, s, _ = proj.shape
    nq = s // Q_BLOCK
    return pl.pallas_call(
        _swa_body,
        grid=(b, nq),
        in_specs=[
            pl.BlockSpec(memory_space=pltpu.SMEM),
            pl.BlockSpec((1, Q_BLOCK, SWA_WIDTH), lambda i, j: (i, j, COL_SWA_Q // SWA_WIDTH)),
            pl.BlockSpec((1, s, SWA_KV_WIDTH), lambda i, j: (i, 0, COL_SWA_K // SWA_KV_WIDTH)),
            pl.BlockSpec((1, s, SWA_KV_WIDTH), lambda i, j: (i, 0, COL_SWA_V // SWA_KV_WIDTH)),
        ],
        out_specs=pl.BlockSpec((1, Q_BLOCK, SWA_WIDTH), lambda i, j: (i, j, 0)),
        out_shape=jax.ShapeDtypeStruct((b, s, SWA_WIDTH), BF16),
        compiler_params=_cparams(("parallel", "arbitrary")),
        name="swa_attention",
    )(sinks, proj, proj, proj)


GDN_PAIR = 2 * GDN_CHUNK
GDN_STEP_HEADS = 2
GDN_PREP_PAIRS = 4


def _gdn_body(q_ref, k_ref, v_ref, z_ref, ar_ref, br_ref, cwq_ref, cwk_ref, cwv_ref, alog_ref, dtb_ref,
              nw_ref, o_ref, qn_s, kn_s, v_s, u_s, w_s, qd_s, kdt_s, at_s, eg_s, gr_s, br_s):
    s = q_ref.shape[1]
    n_pairs = s // GDN_PAIR
    heads = range(GDN_STEP_HEADS)
    rowi = lax.broadcasted_iota(jnp.int32, (s, 1), 0)

    def conv_silu(x, w):
        y = x * w[GDN_CONV - 1:GDN_CONV, :]
        for j in range(GDN_CONV - 1):
            sh = GDN_CONV - 1 - j
            xs = jnp.where(rowi >= sh, pltpu.roll(x, sh, axis=0), 0.0)
            y = y + xs * w[j:j + 1, :]
        return _silu(y)

    def l2n(t):
        return t * lax.rsqrt(jnp.sum(t * t, axis=-1, keepdims=True) + NORM_EPS)

    for hh in heads:
        cols = slice(hh * HEAD_DIM, (hh + 1) * HEAD_DIM)
        qn_s[hh] = l2n(conv_silu(q_ref[0, :, cols], cwq_ref[:, cols])) * ATTN_SCALE
        kn_s[hh] = l2n(conv_silu(k_ref[0, :, cols], cwk_ref[:, cols]))
        v_s[hh] = conv_silu(v_ref[0, :, cols], cwv_ref[:, cols])
        a_rate = jnp.exp(alog_ref[hh])
        x = ar_ref[0, hh] + dtb_ref[hh]
        gr_s[hh] = -(a_rate * (jnp.maximum(x, 0.0) + jnp.log(1.0 + jnp.exp(-jnp.abs(x)))))
        br_s[hh] = jax.nn.sigmoid(br_ref[0, hh])

    ii = lax.broadcasted_iota(jnp.int32, (GDN_PAIR, GDN_PAIR), 0)
    jj = lax.broadcasted_iota(jnp.int32, (GDN_PAIR, GDN_PAIR), 1)
    same = _div(ii, GDN_CHUNK) == _div(jj, GDN_CHUNK)
    incl = same & (ii >= jj)
    strict = same & (ii > jj)
    diag = ii == jj
    eye = jnp.where(diag, 1.0, 0.0)
    last_of = same & (_mod(jj, GDN_CHUNK) == GDN_CHUNK - 1)
    n_double = int(np.log2(GDN_CHUNK)) - 1

    def tile_rows(p):
        start = p * GDN_PAIR
        return pl.ds(start if isinstance(start, int) else pl.multiple_of(start, GDN_PAIR), GDN_PAIR)

    def prepare_stages(grp):
        chains = [(hh, grp * GDN_PREP_PAIRS + pp) for pp in range(GDN_PREP_PAIRS) for hh in heads]
        rows = [tile_rows(p) for _, p in chains]
        n = range(len(chains))
        beta, gc_col, gl_col, decay = [], [], [], []
        for hh, p in chains:
            g_row = gr_s[hh, pl.ds(p, 1), :]
            beta.append(jnp.sum(jnp.where(diag, br_s[hh, pl.ds(p, 1), :], 0.0), axis=1, keepdims=True))
            gcc = jnp.sum(jnp.where(incl, g_row, 0.0), axis=1, keepdims=True)
            gcr = jnp.sum(jnp.where(diag, gcc, 0.0), axis=0, keepdims=True)
            gc_col.append(gcc)
            gl_col.append(jnp.sum(jnp.where(last_of, gcr, 0.0), axis=1, keepdims=True))
            decay.append(jnp.exp(jnp.where(incl, gcc - gcr, NEG_INF)))
        q = [qn_s[hh, rows[c], :] for c, (hh, _) in enumerate(chains)]
        k = [kn_s[hh, rows[c], :] for c, (hh, _) in enumerate(chains)]
        v = [v_s[hh, rows[c], :] for c, (hh, _) in enumerate(chains)]
        kb = [k[c] * beta[c] for c in n]
        kf = [k[c].astype(BF16) for c in n]
        kk = [_dot_nt(kb[c].astype(BF16), kf[c]) for c in n]
        qk = [_dot_nt(q[c].astype(BF16), kf[c]) for c in n]
        yield
        pw = [-jnp.where(strict, kk[c] * decay[c], 0.0) for c in n]
        tinv = [eye + pw[c] for c in n]
        for _ in range(n_double):
            pw = [_dot_hi(pw[c], pw[c]) for c in n]
            yield
            tinv = [tinv[c] + _dot_hi(tinv[c], pw[c]) for c in n]
            yield
        egc = [jnp.exp(gc_col[c]) for c in n]
        sol = [_dot_hi(tinv[c], jnp.concatenate([v[c] * beta[c], kb[c] * egc[c]], axis=1)) for c in n]
        yield
        for c, (hh, _) in enumerate(chains):
            u_s[hh, rows[c], :] = sol[c][:, :HEAD_DIM]
            w_s[hh, rows[c], :] = sol[c][:, HEAD_DIM:]
            at_s[hh, rows[c], :] = jnp.where(incl, qk[c] * decay[c], 0.0)
            qd_s[hh, rows[c], :] = q[c] * egc[c]
            kdt_s[hh, rows[c], :] = (k[c] * jnp.exp(gl_col[c] - gc_col[c])).T
            eg_s[hh, rows[c], :] = jnp.broadcast_to(jnp.exp(gl_col[c]), (GDN_PAIR, HEAD_DIM))

    rin = lax.broadcasted_iota(jnp.int32, (GDN_PAIR, 1), 0)
    nw = nw_ref[...]

    def scan_stages(grp, states):
        for pp in range(GDN_PREP_PAIRS):
            rows = tile_rows(grp * GDN_PREP_PAIRS + pp)
            u = [u_s[hh, rows, :] for hh in heads]
            w = [w_s[hh, rows, :].astype(BF16) for hh in heads]
            qd = [qd_s[hh, rows, :].astype(BF16) for hh in heads]
            at = [at_s[hh, rows, :].astype(BF16) for hh in heads]
            kdt = [kdt_s[hh, rows, :].astype(BF16) for hh in heads]
            eg = [eg_s[hh, rows, :] for hh in heads]
            o = [jnp.zeros((GDN_PAIR, HEAD_DIM), F32) for _ in heads]
            for c in range(GDN_PAIR // GDN_CHUNK):
                in_c = _div(rin, GDN_CHUNK) == c
                sb = [states[hh].astype(BF16) for hh in heads]
                ws = [_dot(w[hh], sb[hh]) for hh in heads]
                qs = [_dot(qd[hh], sb[hh]) for hh in heads]
                yield
                vb = [jnp.where(in_c, u[hh] - ws[hh], 0.0).astype(BF16) for hh in heads]
                av = [_dot(at[hh], vb[hh]) for hh in heads]
                kv = [_dot(kdt[hh], vb[hh]) for hh in heads]
                yield
                o = [jnp.where(in_c, qs[hh] + av[hh], o[hh]) for hh in heads]
                for hh in heads:
                    states[hh] = states[hh] * eg[hh][c * GDN_CHUNK:c * GDN_CHUNK + 1, :] + kv[hh]
            for hh in heads:
                cols = slice(hh * HEAD_DIM, (hh + 1) * HEAD_DIM)
                on = o[hh] * lax.rsqrt(jnp.mean(o[hh] * o[hh], axis=-1, keepdims=True) + NORM_EPS) * nw
                o_ref[0, rows, cols] = (on * _silu(z_ref[0, rows, cols])).astype(o_ref.dtype)

    def alternate(*phases):
        live = list(phases)
        while live:
            for ph in list(live):
                if next(ph, live) is live:
                    live.remove(ph)

    n_groups = n_pairs // GDN_PREP_PAIRS
    alternate(prepare_stages(0))

    def group_step(grp, states):
        states = list(states)
        alternate(prepare_stages(grp), scan_stages(grp - 1, states))
        return tuple(states)

    states = lax.fori_loop(1, n_groups, group_step, tuple(jnp.zeros((HEAD_DIM, HEAD_DIM), F32) for _ in heads))
    alternate(scan_stages(n_groups - 1, list(states)))


def _gdn_mixer(proj, conv_w, a_log, dt_bias, norm_w):
    b, s, _ = proj.shape
    h = GDN_HEADS
    nh = GDN_STEP_HEADS
    n_pairs = s // GDN_PAIR
    wid = nh * HEAD_DIM

    def rows_of(col):
        return proj[:, :, col:col + h].transpose(0, 2, 1).reshape(b, h, n_pairs, GDN_PAIR)

    seq = lambda off: pl.BlockSpec((1, s, wid), lambda i, j: (i, 0, (COL_GDN + off * GDN_WIDTH) // wid + j))
    gate = pl.BlockSpec((1, nh, n_pairs, GDN_PAIR), lambda i, j: (i, j, 0, 0))
    cw = lambda off: pl.BlockSpec((GDN_CONV, wid), lambda i, j: (0, off * (h // nh) + j))
    scalar = pl.BlockSpec((nh, 1, 1), lambda i, j: (j, 0, 0))
    seq_f32 = pltpu.VMEM((nh, s, HEAD_DIM), F32)
    row_f32 = pltpu.VMEM((nh, n_pairs, GDN_PAIR), F32)
    return pl.pallas_call(
        _gdn_body,
        grid=(b, h // nh),
        in_specs=[
            seq(0), seq(1), seq(2), seq(3),
            gate, gate,
            cw(0), cw(1), cw(2),
            scalar, scalar,
            pl.BlockSpec((1, HEAD_DIM), lambda i, j: (0, 0)),
        ],
        out_specs=pl.BlockSpec((1, s, wid), lambda i, j: (i, 0, j)),
        out_shape=jax.ShapeDtypeStruct((b, s, GDN_WIDTH), BF16),
        scratch_shapes=[seq_f32] * 9 + [row_f32, row_f32],
        compiler_params=_cparams(("parallel", "arbitrary")),
        name="gdn_mixer",
    )(proj, proj, proj, proj, rows_of(COL_SMALL + SMALL_A), rows_of(COL_SMALL + SMALL_B), conv_w, conv_w, conv_w,
      a_log.reshape(h, 1, 1), dt_bias.reshape(h, 1, 1), norm_w.reshape(1, HEAD_DIM))


def _outproj_body(on_ref, og_ref, os_ref, x_ref, w_ref, o_ref):
    mix = jnp.concatenate([on_ref[...], og_ref[...], os_ref[...]], axis=1)
    o_ref[...] = x_ref[...] + _dot(mix, w_ref[...])


def _outproj_router_body(on_ref, og_ref, os_ref, x_ref, w_ref, nw_ref, r_ref, o_ref, h_ref, lg_ref):
    mix = jnp.concatenate([on_ref[...], og_ref[...], os_ref[...]], axis=1)
    x1 = x_ref[...] + _dot(mix, w_ref[...])
    o_ref[...] = x1
    h = _rms(x1, nw_ref[...])
    h_ref[...] = h
    lg_ref[...] = _dot_hi(h, r_ref[...])


def _out_proj(o_nsa, o_gdn, o_swa, x2d, w_bf16, ffn_norm=None, router=None, tm=512):
    n, d = x2d.shape
    row = lambda wid: pl.BlockSpec((tm, wid), lambda i: (i, 0))
    full = lambda a, c: pl.BlockSpec((a, c), lambda i: (0, 0))
    in_specs = [row(NSA_WIDTH), row(GDN_WIDTH), row(SWA_WIDTH), row(d), full(d, d)]
    args = [o_nsa, o_gdn, o_swa, x2d, w_bf16]
    if router is None:
        return pl.pallas_call(
            _outproj_body, grid=(n // tm,), in_specs=in_specs, out_specs=row(d),
            out_shape=jax.ShapeDtypeStruct((n, d), F32),
            compiler_params=_cparams(("parallel",)), name="out_proj",
        )(*args)
    router_pad = jnp.pad(router, ((0, 0), (0, LANES - router.shape[1])))
    return pl.pallas_call(
        _outproj_router_body, grid=(n // tm,),
        in_specs=in_specs + [full(1, d), full(d, LANES)],
        out_specs=[row(d), row(d), row(LANES)],
        out_shape=[jax.ShapeDtypeStruct((n, d), F32), jax.ShapeDtypeStruct((n, d), F32),
                   jax.ShapeDtypeStruct((n, LANES), F32)],
        compiler_params=_cparams(("parallel",)), name="out_proj_router",
    )(*args, ffn_norm.reshape(1, d), router_pad)


def _swiglu_tile(h, wg_parts, wu_parts, wd_parts):
    dk = h.shape[1] // len(wg_parts)
    hk = [h[:, k * dk:(k + 1) * dk] for k in range(len(wg_parts))]
    a = sum(_dot(hk[k], w.astype(BF16)) for k, w in enumerate(wg_parts))
    b = sum(_dot(hk[k], w.astype(BF16)) for k, w in enumerate(wu_parts))
    t = (_silu(a) * b).astype(BF16)
    return jnp.concatenate([_dot(t, w.astype(BF16)) for w in wd_parts], axis=1)


def _split_weight_specs(d, tf, up_map, down_map):
    dk = d // W_SPLIT
    up = [pl.BlockSpec((dk, tf), up_map(k)) for k in range(W_SPLIT)]
    down = [pl.BlockSpec((tf, dk), down_map(k)) for k in range(W_SPLIT)]
    return up + up + down


def _dense_body(x_ref, nw_ref, *refs):
    w_refs, (o_ref, h_ref) = refs[:3 * W_SPLIT], refs[3 * W_SPLIT:]

    @pl.when(pl.program_id(1) == 0)
    def _():
        x = x_ref[...]
        h_ref[...] = _rms(x, nw_ref[...]).astype(BF16)
        o_ref[...] = x

    parts = [[r[...] for r in w_refs[j * W_SPLIT:(j + 1) * W_SPLIT]] for j in range(3)]
    o_ref[...] += _swiglu_tile(h_ref[...], *parts)


def _dense_ffn(x2d, norm_w, wg, wu, wd, tm=1024, tf=256):
    n, d = x2d.shape
    ff = wg.shape[1]
    return pl.pallas_call(
        _dense_body,
        grid=(n // tm, ff // tf),
        in_specs=[
            pl.BlockSpec((tm, d), lambda i, f: (i, 0), pipeline_mode=pl.Buffered(1)),
            pl.BlockSpec((1, d), lambda i, f: (0, 0)),
        ] + _split_weight_specs(d, tf, lambda k: (lambda i, f: (k, f)), lambda k: (lambda i, f: (f, k))),
        out_specs=pl.BlockSpec((tm, d), lambda i, f: (i, 0)),
        out_shape=jax.ShapeDtypeStruct((n, d), F32),
        scratch_shapes=[pltpu.VMEM((tm, d), BF16)],
        compiler_params=_cparams(("parallel", "arbitrary")),
        name="dense_ffn",
    )(x2d, norm_w.reshape(1, d), *([wg] * W_SPLIT + [wu] * W_SPLIT + [wd] * W_SPLIT))


def _row_gather(idx_ref, base, count, src_hbm, dst, sem):
    def body(r, carry):
        t = idx_ref[base + r]
        pltpu.make_async_copy(src_hbm.at[pl.ds(t, 1), :], dst.at[pl.ds(r, 1), :], sem).start()
        return carry

    lax.fori_loop(0, count, body, 0)


def _row_gather_wait(count, src_hbm, dst, sem):
    pltpu.make_async_copy(src_hbm.at[pl.ds(0, count), :], dst, sem).wait()


MOE_STEP_ROWS = 32


def _moe_body(be_ref, nu_ref, br_ref, tok_ref, h_hbm, *refs, nf):
    w_refs, (o_ref, xbuf, x_s, sem) = refs[:3 * W_SPLIT], refs[3 * W_SPLIT:]
    i = pl.program_id(0)
    f = pl.program_id(1)
    tm = x_s.shape[0]
    n_used = nu_ref[0]
    used = i < n_used
    more = i + 1 < n_used
    small = br_ref[i] <= tm // 2
    head_rows = tm - MOE_STEP_ROWS * nf
    assert head_rows >= 0

    @pl.when(f == 0)
    def _():
        o_ref[...] = jnp.zeros_like(o_ref)

        @pl.when(i == 0)
        def _():
            _row_gather(tok_ref, 0, tm, h_hbm, xbuf, sem)

        @pl.when(used)
        def _():
            _row_gather_wait(tm, h_hbm, xbuf, sem)
            x_s[...] = xbuf[...].astype(BF16)

        @pl.when(more)
        def _():
            _row_gather(tok_ref, (i + 1) * tm, head_rows, h_hbm, xbuf, sem)

    def accumulate(rows):
        parts = [[r[0] for r in w_refs[j * W_SPLIT:(j + 1) * W_SPLIT]] for j in range(3)]
        o_ref[0:rows, :] += _swiglu_tile(x_s[0:rows, :], *parts)

    def fetch_next_rows():
        base = head_rows + f * MOE_STEP_ROWS
        for u in range(MOE_STEP_ROWS):
            t = tok_ref[(i + 1) * tm + base + u]
            pltpu.make_async_copy(h_hbm.at[pl.ds(t, 1), :], xbuf.at[pl.ds(base + u, 1), :], sem).start()

    for rows, fits in ((tm // 2, small), (tm, jnp.logical_not(small))):
        @pl.when(more & fits)
        def _(rows=rows):
            fetch_next_rows()
            accumulate(rows)

        @pl.when(used & jnp.logical_not(more) & fits)
        def _(rows=rows):
            accumulate(rows)


def _moe_ffn(h, row_tok, blk_expert, n_used, blk_rows, wg, wu, wd, tm, tf=256):
    p = row_tok.shape[0]
    d = h.shape[1]
    ff = wg.shape[2]
    nb = p // tm
    nf = ff // tf

    def fsel(i, f, nu):
        return jnp.where(i < nu[0], f, nf - 1)

    dk = d // W_SPLIT
    up = [pl.BlockSpec((1, dk, tf), lambda i, f, be, nu, br, tok, k=k: (be[i], k, fsel(i, f, nu)))
          for k in range(W_SPLIT)]
    down = [pl.BlockSpec((1, tf, dk), lambda i, f, be, nu, br, tok, k=k: (be[i], fsel(i, f, nu), k))
            for k in range(W_SPLIT)]
    grid_spec = pltpu.PrefetchScalarGridSpec(
        num_scalar_prefetch=4,
        grid=(nb, nf),
        in_specs=[pl.BlockSpec(memory_space=pl.ANY)] + up + up + down,
        out_specs=pl.BlockSpec((tm, d), lambda i, f, be, nu, br, tok: (i, 0)),
        scratch_shapes=[pltpu.VMEM((tm, d), F32), pltpu.VMEM((tm, d), BF16), pltpu.SemaphoreType.DMA(())],
    )
    return pl.pallas_call(
        functools.partial(_moe_body, nf=nf),
        grid_spec=grid_spec,
        out_shape=jax.ShapeDtypeStruct((p, d), F32),
        compiler_params=_cparams(("arbitrary", "arbitrary")),
        name="moe_ffn",
    )(blk_expert, n_used, blk_rows, row_tok, h, *([wg] * W_SPLIT + [wu] * W_SPLIT + [wd] * W_SPLIT))


def _moe_layer(h, logits, wg, wu, wd, tm=1024):
    n = h.shape[0]
    top_logit, top_idx = lax.top_k(logits, TOP_K)
    gate = jax.nn.softmax(top_logit, axis=-1)
    a = n * TOP_K
    e_flat = top_idx.reshape(a)
    onehot = (e_flat[:, None] == jnp.arange(N_EXPERTS, dtype=e_flat.dtype)[None]).astype(jnp.int32)
    rank = jnp.sum((jnp.cumsum(onehot, axis=0) - onehot) * onehot, axis=1)
    counts = jnp.sum(onehot, axis=0)
    padded = (counts + tm - 1) // tm * tm
    pad_end = jnp.cumsum(padded)
    pad_start = pad_end - padded
    dest = (pad_start[e_flat] + rank).astype(jnp.int32)
    nb = a // tm + N_EXPERTS
    n_used = (pad_end[-1] // tm).astype(jnp.int32)
    blk = jnp.minimum(jnp.arange(nb, dtype=jnp.int32), n_used - 1)
    blk_expert = jnp.sum((blk[:, None] * tm >= pad_end[None, :]).astype(jnp.int32), axis=1)
    blk_expert = jnp.minimum(blk_expert, N_EXPERTS - 1)
    blk_rows = jnp.clip(counts[blk_expert] - (blk * tm - pad_start[blk_expert]), 0, tm).astype(jnp.int32)
    row_tok = jnp.zeros((nb * tm,), jnp.int32).at[dest].set(jnp.arange(a, dtype=jnp.int32) // TOP_K)
    ys = _moe_ffn(h, row_tok, blk_expert, n_used.reshape(1), blk_rows, wg, wu, wd, tm)
    return ys, dest, gate


def _combine_body(d_ref, x_ref, g_ref, w_ref, ys_hbm, o_ref, buf, sem, *, final_norm):
    j = pl.program_id(0)
    tt = x_ref.shape[0]
    slot = j % 2

    def start(step, sl):
        for k in range(TOP_K):
            _row_gather(d_ref, (step * TOP_K + k) * tt, tt, ys_hbm, buf.at[sl, k], sem.at[sl])

    @pl.when(j == 0)
    def _():
        start(0, 0)

    for k in range(TOP_K):
        _row_gather_wait(tt, ys_hbm, buf.at[slot, k], sem.at[slot])

    @pl.when(j + 1 < pl.num_programs(0))
    def _():
        start(j + 1, 1 - slot)

    g = g_ref[...]
    f = buf[slot, 0] * g[:, 0:1]
    for k in range(1, TOP_K):
        f = f + buf[slot, k] * g[:, k:k + 1]
    y = x_ref[...] + f
    o_ref[...] = _rms(y, w_ref[...]) if final_norm else y


def _moe_combine(x2d, ys, dest, gate, norm_w, final_norm, tt=256):
    n, d = x2d.shape
    steps = n // tt
    order = dest.reshape(steps, tt, TOP_K).transpose(0, 2, 1).reshape(n * TOP_K)
    grid_spec = pltpu.PrefetchScalarGridSpec(
        num_scalar_prefetch=1,
        grid=(steps,),
        in_specs=[
            pl.BlockSpec((tt, d), lambda j, dr: (j, 0)),
            pl.BlockSpec((tt, TOP_K), lambda j, dr: (j, 0)),
            pl.BlockSpec((1, d), lambda j, dr: (0, 0)),
            pl.BlockSpec(memory_space=pl.ANY),
        ],
        out_specs=pl.BlockSpec((tt, d), lambda j, dr: (j, 0)),
        scratch_shapes=[pltpu.VMEM((2, TOP_K, tt, d), F32), pltpu.SemaphoreType.DMA((2,))],
    )
    return pl.pallas_call(
        functools.partial(_combine_body, final_norm=final_norm),
        grid_spec=grid_spec,
        out_shape=jax.ShapeDtypeStruct((n, d), F32),
        compiler_params=_cparams(("arbitrary",)),
        name="moe_combine",
    )(order, x2d, gate, norm_w.reshape(1, d), ys)


def kernel(x, attn_norm, w_in, cmp_pos_k, cmp_pos_v, cmp_w1_k, cmp_w2_k, cmp_w1_v, cmp_w2_v, gdn_conv_w, gdn_a_log, gdn_dt_bias, gdn_norm_w, swa_sinks, w_out, ffn_norm, dense_w_gate, dense_w_up, dense_w_down, moe_router, moe_w_gate, moe_w_up, moe_w_down, final_norm):
    b, s, d = x.shape
    n = b * s
    depth = w_in.shape[0]
    assert depth % 2 == 0, "the trunk ends on an expert layer, whose combine feeds the final norm"
    x2d = x.reshape(n, d)
    for layer in range(depth):
        proj = _in_proj(x2d, attn_norm[layer], _reorder_w_in(w_in, layer)).reshape(b, s, PROJ_COLS)
        n_strides = s // CMP_STRIDE
        ck = proj[:, :, COL_NSA_KV:COL_NSA_KV + HEAD_DIM].reshape(b, n_strides, CMP_STRIDE * HEAD_DIM)
        cv = proj[:, :, COL_NSA_KV + HEAD_DIM:COL_NSA_KV + 2 * HEAD_DIM].reshape(b, n_strides, CMP_STRIDE * HEAD_DIM)
        kc, vc = _nsa_compress(ck, cv, cmp_pos_k[layer], cmp_pos_v[layer], cmp_w1_k[layer], cmp_w2_k[layer],
                               cmp_w1_v[layer], cmp_w2_v[layer])
        o_nsa = _nsa_attention(proj, kc, vc).reshape(n, NSA_WIDTH)
        o_gdn = _gdn_mixer(proj, gdn_conv_w[layer], gdn_a_log[layer], gdn_dt_bias[layer],
                           gdn_norm_w[layer]).reshape(n, GDN_WIDTH)
        o_swa = _swa_attention(proj, swa_sinks[layer]).reshape(n, SWA_WIDTH)
        w_o = w_out[layer].astype(BF16)
        i = layer // 2
        if layer % 2 == 0:
            x1 = _out_proj(o_nsa, o_gdn, o_swa, x2d, w_o)
            x2d = _dense_ffn(x1, ffn_norm[layer], dense_w_gate[i], dense_w_up[i], dense_w_down[i])
        else:
            x1, h2, logits = _out_proj(o_nsa, o_gdn, o_swa, x2d, w_o, ffn_norm[layer], moe_router[i])
            ys, dest, gate = _moe_layer(h2, logits[:, :N_EXPERTS], moe_w_gate[i], moe_w_up[i], moe_w_down[i])
            x2d = _moe_combine(x1, ys, dest, gate, final_norm, final_norm=layer == depth - 1)
    return x2d.reshape(b, s, d)
```

```python
import functools

import jax
import jax.numpy as jnp
import numpy as np
from jax import lax
from jax.experimental import pallas as pl
from jax.experimental.pallas import tpu as pltpu

F32 = jnp.float32
BF16 = jnp.bfloat16

D_MODEL = 2048
HEAD_DIM = 128
NSA_HEADS = 4
SWA_HEADS = 4
GDN_HEADS = 8
SWA_KV_HEADS = 2
NSA_WIDTH = NSA_HEADS * HEAD_DIM
GDN_WIDTH = GDN_HEADS * HEAD_DIM
SWA_WIDTH = SWA_HEADS * HEAD_DIM
SWA_KV_WIDTH = SWA_KV_HEADS * HEAD_DIM

CMP_BLOCK = 32
CMP_STRIDE = 16
SEL_BLOCK = 64
SEL_TOPN = 8
NSA_WINDOW = 512
FORCE_BONUS = 1000.0
GDN_CONV = 4
GDN_CHUNK = 64
SWA_WINDOW = 128
Q_BLOCK = 128
N_EXPERTS = 8
TOP_K = 2
NORM_EPS = 1e-6
NEG_INF = -1e30
ATTN_SCALE = HEAD_DIM ** -0.5

LANES = 128
VMEM_LIMIT_BYTES = 56 * 1024 * 1024

COL_NSA_Q = 0
COL_SWA_Q = 512
COL_GDN = 1024
COL_NSA_KV = COL_GDN + 4 * GDN_WIDTH
COL_SWA_K = COL_NSA_KV + 6 * HEAD_DIM
COL_SWA_V = COL_SWA_K + SWA_KV_WIDTH
COL_SMALL = COL_SWA_V + SWA_KV_WIDTH
PROJ_COLS = 6656
PROJ_TN = 1664
SMALL_A = 3 * NSA_HEADS
SMALL_B = SMALL_A + GDN_HEADS


def _alibi_slopes():
    n = NSA_HEADS + SWA_HEADS
    s = [2.0 ** (-8.0 * i / n) for i in range(1, n + 1)]
    return tuple(s[SWA_HEADS:]), tuple(s[:SWA_HEADS])


NSA_SLOPES, SWA_SLOPES = _alibi_slopes()


def _div(v, c):
    assert c & (c - 1) == 0
    return v >> (c.bit_length() - 1)


def _mod(v, c):
    assert c & (c - 1) == 0
    return v & (c - 1)


def _cparams(sem):
    return pltpu.CompilerParams(dimension_semantics=sem, vmem_limit_bytes=VMEM_LIMIT_BYTES)


def _rms(x, w):
    return x * lax.rsqrt(jnp.mean(x * x, axis=-1, keepdims=True) + NORM_EPS) * w


def _silu(x):
    return x * jax.nn.sigmoid(x)


def _dot(a, b):
    return jnp.dot(a, b, preferred_element_type=F32)


def _dot_nt(a, b):
    return lax.dot_general(a, b, (((1,), (1,)), ((), ())), preferred_element_type=F32)


def _split2(a):
    hi = a.astype(BF16)
    lo = (a - hi.astype(F32)).astype(BF16)
    return hi, lo


def _dot_hi(a, b):
    ah, al = _split2(a)
    bh, bl = _split2(b)
    return _dot(ah, bh) + (_dot(ah, bl) + _dot(al, bh))


def _dot_exact_rhs(a, b_bf16):
    a0 = a.astype(BF16)
    r = a - a0.astype(F32)
    a1 = r.astype(BF16)
    a2 = (r - a1.astype(F32)).astype(BF16)
    return _dot(a0, b_bf16) + (_dot(a1, b_bf16) + _dot(a2, b_bf16))


def _inproj_body(x_ref, nw_ref, w_ref, o_ref, h_ref):
    @pl.when(pl.program_id(1) == 0)
    def _():
        h_ref[...] = _rms(x_ref[...], nw_ref[...]).astype(BF16)

    o_ref[...] = _dot(h_ref[...], w_ref[...])


def _in_proj(x2d, norm_w, w_all, layer, tm=1024):
    n, d = x2d.shape
    c = w_all.shape[2]
    return pl.pallas_call(
        _inproj_body,
        grid=(n // tm, c // PROJ_TN),
        in_specs=[
            pl.BlockSpec((tm, d), lambda i, j: (i, 0)),
            pl.BlockSpec((1, d), lambda i, j: (0, 0)),
            pl.BlockSpec((None, d, PROJ_TN), lambda i, j: (layer, 0, j)),
        ],
        out_specs=pl.BlockSpec((tm, PROJ_TN), lambda i, j: (i, j)),
        out_shape=jax.ShapeDtypeStruct((n, c), F32),
        scratch_shapes=[pltpu.VMEM((tm, d), BF16)],
        compiler_params=_cparams(("parallel", "arbitrary")),
        name="in_proj",
    )(x2d, norm_w.reshape(1, d), w_all)


def _w_in_segments():
    src = {}
    o = 0
    for name, size in (("nsa_q", NSA_WIDTH), ("nsa_kv", 6 * HEAD_DIM), ("nsa_g", 3 * NSA_HEADS),
                       ("gdn", 4 * GDN_WIDTH), ("ga", GDN_HEADS), ("gb", GDN_HEADS),
                       ("swa_q", SWA_WIDTH), ("swa_k", SWA_KV_WIDTH), ("swa_v", SWA_KV_WIDTH)):
        src[name] = (o, size)
        o += size
    dst = {"nsa_q": COL_NSA_Q, "swa_q": COL_SWA_Q, "gdn": COL_GDN, "nsa_kv": COL_NSA_KV, "swa_k": COL_SWA_K,
           "swa_v": COL_SWA_V, "nsa_g": COL_SMALL, "ga": COL_SMALL + SMALL_A, "gb": COL_SMALL + SMALL_B}
    return [(src[k][0], dst[k], src[k][1]) for k in src], o


def _reorder_body(start_ref, wt_hbm, small_ref, o_ref, buf, sem, *, n_main):
    j = pl.program_id(0)
    slot = j % 2
    layers = o_ref.shape[0]

    def fetch(blk, sl):
        return pltpu.make_async_copy(wt_hbm.at[pl.ds(start_ref[blk], LANES)], buf.at[sl], sem.at[sl])

    @pl.when(j == 0)
    def _():
        fetch(0, 0).start()

    @pl.when(j < n_main)
    def _():
        fetch(j, slot).wait()

    @pl.when(j + 1 < n_main)
    def _():
        fetch(j + 1, 1 - slot).start()

    @pl.when(j < n_main)
    def _():
        for l in range(layers):
            o_ref[l] = buf[slot, :, l, :].T.astype(o_ref.dtype)

    @pl.when(j == n_main)
    def _():
        for l in range(layers):
            o_ref[l] = small_ref[:, l, :].T.astype(o_ref.dtype)

    @pl.when(j > n_main)
    def _():
        o_ref[...] = jnp.zeros_like(o_ref)


def _reorder_w_in(w_all):
    layers, d, c = w_all.shape
    segs, total = _w_in_segments()
    assert total == c
    wt = jnp.transpose(w_all, (2, 0, 1))
    n_main = COL_SMALL // LANES
    starts, small, small_at = [None] * n_main, [], COL_SMALL
    for s0, d0, wid in segs:
        if wid % LANES == 0:
            assert d0 % LANES == 0
            for i in range(wid // LANES):
                starts[d0 // LANES + i] = s0 + i * LANES
        else:
            assert d0 == small_at, "the narrow segments fill the last block back to back"
            small.append(wt[s0:s0 + wid])
            small_at += wid
    assert None not in starts
    small = jnp.concatenate(small + [jnp.zeros((COL_SMALL + LANES - small_at, layers, d), wt.dtype)], axis=0)
    grid_spec = pltpu.PrefetchScalarGridSpec(
        num_scalar_prefetch=1,
        grid=(PROJ_COLS // LANES,),
        in_specs=[pl.BlockSpec(memory_space=pl.ANY),
                  pl.BlockSpec((LANES, layers, d), lambda j, st: (0, 0, 0))],
        out_specs=pl.BlockSpec((layers, d, LANES), lambda j, st: (0, 0, j)),
        scratch_shapes=[pltpu.VMEM((2, LANES, layers, d), F32), pltpu.SemaphoreType.DMA((2,))],
    )
    return pl.pallas_call(
        functools.partial(_reorder_body, n_main=n_main),
        grid_spec=grid_spec,
        out_shape=jax.ShapeDtypeStruct((layers, d, PROJ_COLS), BF16),
        compiler_params=_cparams(("arbitrary",)),
        name="reorder_w_in",
    )(jnp.asarray(starts, jnp.int32), wt, small)


KEY_TILE = 256


def _key_tile(ref, col0=0):
    return lambda kt: ref[0, pl.ds(pl.multiple_of(kt * KEY_TILE, KEY_TILE), KEY_TILE), col0:col0 + HEAD_DIM]


def _flash(qs, get_k, get_v, lo, hi, t_col, slope_col, mask_fn):
    r = qs.shape[0]
    lane = lax.broadcasted_iota(jnp.int32, (r, KEY_TILE), 1)

    def body(kt, carry):
        m, l, acc = carry
        k = get_k(kt).astype(BF16)
        v = get_v(kt).astype(BF16)
        s = _dot_nt(qs, k) * ATTN_SCALE
        dist = t_col - (kt * KEY_TILE + lane)
        mask = mask_fn(kt, dist)
        logits = jnp.where(mask, s - slope_col * dist.astype(F32), NEG_INF)
        m_new = jnp.maximum(m, jnp.max(logits, axis=-1, keepdims=True))
        alpha = jnp.exp(m - m_new)
        p = jnp.where(mask, jnp.exp(logits - m_new), 0.0)
        l = alpha * l + jnp.sum(p, axis=-1, keepdims=True)
        acc = alpha * acc + _dot(p.astype(BF16), v)
        return m_new, l, acc

    init = (jnp.full((r, 1), NEG_INF, F32), jnp.zeros((r, 1), F32), jnp.zeros((r, HEAD_DIM), F32))
    return lax.fori_loop(lo, hi, body, init)


def _stack_heads(x, heads):
    return jnp.concatenate([x[:, h * HEAD_DIM:(h + 1) * HEAD_DIM] for h in heads], axis=0)


def _head_const_col(rows, values):
    head = _div(lax.broadcasted_iota(jnp.int32, (rows, 1), 0), Q_BLOCK)
    col = jnp.full((rows, 1), values[-1], F32)
    for h in range(len(values) - 2, -1, -1):
        col = jnp.where(head == h, values[h], col)
    return col


def _compress_body(ck_ref, cv_ref, pk_ref, pv_ref, w1k_ref, w2k_ref, w1v_ref, w2v_ref, kc_ref, vc_ref):
    half = CMP_STRIDE * HEAD_DIM

    def run(c_ref, p_ref, w1_ref, w2_ref, o_ref):
        c = c_ref[0]
        lo = _dot((c + p_ref[0:1, :]).astype(BF16), w1_ref[0:half, :].astype(BF16))
        hi = _dot((c + p_ref[1:2, :]).astype(BF16), w1_ref[half:2 * half, :].astype(BF16))
        hid = lo + pltpu.roll(hi, hi.shape[0] - 1, axis=0)
        o_ref[0] = _dot(_silu(hid).astype(BF16), w2_ref[...].astype(BF16))

    run(ck_ref, pk_ref, w1k_ref, w2k_ref, kc_ref)
    run(cv_ref, pv_ref, w1v_ref, w2v_ref, vc_ref)


def _nsa_compress(ck, cv, pos_k, pos_v, w1k, w2k, w1v, w2v):
    b, ns, wid = ck.shape
    hid = w1k.shape[1]
    full = lambda shape: pl.BlockSpec(shape, lambda i: (0,) * len(shape))
    return pl.pallas_call(
        _compress_body,
        grid=(b,),
        in_specs=[
            pl.BlockSpec((1, ns, wid), lambda i: (i, 0, 0)),
            pl.BlockSpec((1, ns, wid), lambda i: (i, 0, 0)),
            full((2, wid)), full((2, wid)),
            full((2 * wid, hid)), full((hid, HEAD_DIM)),
            full((2 * wid, hid)), full((hid, HEAD_DIM)),
        ],
        out_specs=[pl.BlockSpec((1, ns, HEAD_DIM), lambda i: (i, 0, 0))] * 2,
        out_shape=[jax.ShapeDtypeStruct((b, ns, HEAD_DIM), F32)] * 2,
        compiler_params=_cparams(("parallel",)),
        name="nsa_compress",
    )(ck, cv, pos_k.reshape(2, wid), pos_v.reshape(2, wid), w1k, w2k, w1v, w2v)


def _nsa_body(q_ref, gl_ref, kc_ref, vc_ref, ksl_ref, vsl_ref, kw_ref, vw_ref, o_ref, *, n_sel):
    qi = pl.program_id(1)
    h4 = range(NSA_HEADS)
    rows = NSA_HEADS * Q_BLOCK
    qs = _stack_heads(q_ref[0], h4).astype(BF16)
    row = lax.broadcasted_iota(jnp.int32, (rows, 1), 0)
    t_col = qi * Q_BLOCK + _mod(row, Q_BLOCK)
    slope_col = _head_const_col(rows, NSA_SLOPES)

    lane = lax.broadcasted_iota(jnp.int32, (rows, LANES), 1)
    c_dist = t_col - (lane * CMP_STRIDE + CMP_BLOCK - 1)
    c_mask = c_dist >= 0
    sc = _dot_nt(qs, kc_ref[0].astype(BF16)) * ATTN_SCALE
    logits = jnp.where(c_mask, sc - slope_col * c_dist.astype(F32), NEG_INF)
    e = jnp.exp(logits - jnp.max(logits, axis=-1, keepdims=True))
    p = e / jnp.sum(e, axis=-1, keepdims=True)
    p = p * (t_col >= CMP_BLOCK - 1).astype(F32)
    o_cmp = _dot(p.astype(BF16), vc_ref[0].astype(BF16))

    psum = p[0:Q_BLOCK] + p[Q_BLOCK:2 * Q_BLOCK] + p[2 * Q_BLOCK:3 * Q_BLOCK] + p[3 * Q_BLOCK:4 * Q_BLOCK]
    cn = lax.broadcasted_iota(jnp.int32, (LANES, LANES), 0) * CMP_STRIDE
    sj = lax.broadcasted_iota(jnp.int32, (LANES, LANES), 1) * SEL_BLOCK
    ov = jnp.maximum(jnp.minimum(cn + CMP_BLOCK, sj + SEL_BLOCK) - jnp.maximum(cn, sj), 0)
    ov = (ov.astype(F32) * (1.0 / CMP_BLOCK)).astype(BF16)
    imp = _dot_exact_rhs(psum, ov)
    blk = lax.broadcasted_iota(jnp.int32, (Q_BLOCK, LANES), 1)
    tq = qi * Q_BLOCK + lax.broadcasted_iota(jnp.int32, (Q_BLOCK, 1), 0)
    cur = _div(tq, SEL_BLOCK)
    forced = (blk == 0) | (blk == cur) | (blk == cur - 1)
    score = jnp.where(blk <= cur, imp + jnp.where(forced, FORCE_BONUS, 0.0), -1.0)
    score_t = score.T[0:n_sel, :]
    blk_t = lax.broadcasted_iota(jnp.int32, (n_sel, Q_BLOCK), 0)
    rank = jnp.zeros((n_sel, Q_BLOCK), F32)
    for i in range(n_sel):
        s_i = score_t[i:i + 1, :]
        ahead = (s_i > score_t) | ((s_i == score_t) & (blk_t > i))
        rank = rank + jnp.where(ahead, 1.0, 0.0)
    sel_t = jnp.where(rank < float(min(SEL_TOPN, n_sel)), 1.0, 0.0)
    sel_b = jnp.concatenate([sel_t, jnp.zeros((LANES - n_sel, Q_BLOCK), F32)], axis=0).T.astype(BF16)

    ej = lax.broadcasted_iota(jnp.int32, (LANES, KEY_TILE), 0)
    ec = _div(lax.broadcasted_iota(jnp.int32, (LANES, KEY_TILE), 1), SEL_BLOCK)
    per_tile = KEY_TILE // SEL_BLOCK
    q_per_key_tile = KEY_TILE // Q_BLOCK
    last_tile = qi // q_per_key_tile

    def sel_mask(kt, dist):
        expand = jnp.where(ej == kt * per_tile + ec, 1.0, 0.0).astype(BF16)
        member = _dot(sel_b, expand)
        member = jnp.concatenate([member] * NSA_HEADS, axis=0)
        return (member > 0.5) & (dist >= 0)

    m, l, acc = _flash(qs, _key_tile(ksl_ref), _key_tile(vsl_ref), 0, last_tile + 1, t_col, slope_col, sel_mask)
    o_slc = acc / l

    def win_mask(kt, dist):
        return (dist >= 0) & (dist < NSA_WINDOW)

    n_prev = -(-(NSA_WINDOW - 1) // Q_BLOCK)
    first_tile = jnp.maximum(qi - n_prev, 0) // q_per_key_tile
    m, l, acc = _flash(qs, _key_tile(kw_ref), _key_tile(vw_ref), first_tile, last_tile + 1, t_col, slope_col,
                       win_mask)
    o_win = acc / l

    g = jax.nn.sigmoid(gl_ref[0])

    def gate(branch):
        return jnp.concatenate([g[:, branch * NSA_HEADS + h:branch * NSA_HEADS + h + 1] for h in h4], axis=0)

    o = gate(0) * o_cmp + gate(1) * o_slc + gate(2) * o_win
    o_ref[0] = jnp.concatenate([o[h * Q_BLOCK:(h + 1) * Q_BLOCK] for h in h4], axis=1).astype(o_ref.dtype)


def _nsa_attention(proj, kc, vc):
    b, s, _ = proj.shape
    nq = s // Q_BLOCK
    kvb = COL_NSA_KV // HEAD_DIM
    seq = lambda c: pl.BlockSpec((1, s, HEAD_DIM), lambda i, j: (i, 0, c))
    return pl.pallas_call(
        functools.partial(_nsa_body, n_sel=s // SEL_BLOCK),
        grid=(b, nq),
        in_specs=[
            pl.BlockSpec((1, Q_BLOCK, NSA_WIDTH), lambda i, j: (i, j, COL_NSA_Q // NSA_WIDTH)),
            pl.BlockSpec((1, Q_BLOCK, LANES), lambda i, j: (i, j, COL_SMALL // LANES)),
            pl.BlockSpec((1, kc.shape[1], HEAD_DIM), lambda i, j: (i, 0, 0)),
            pl.BlockSpec((1, vc.shape[1], HEAD_DIM), lambda i, j: (i, 0, 0)),
            seq(kvb + 2), seq(kvb + 3), seq(kvb + 4), seq(kvb + 5),
        ],
        out_specs=pl.BlockSpec((1, Q_BLOCK, NSA_WIDTH), lambda i, j: (i, j, 0)),
        out_shape=jax.ShapeDtypeStruct((b, s, NSA_WIDTH), BF16),
        compiler_params=_cparams(("parallel", "arbitrary")),
        name="nsa_attention",
    )(proj, proj, kc, vc, proj, proj, proj, proj)


def _swa_body(sink_ref, q_ref, k_ref, v_ref, o_ref):
    qi = pl.program_id(1)
    rep = SWA_HEADS // SWA_KV_HEADS
    rows = rep * Q_BLOCK
    row = lax.broadcasted_iota(jnp.int32, (rows, 1), 0)
    t_col = qi * Q_BLOCK + _mod(row, Q_BLOCK)
    n_prev = -(-(SWA_WINDOW - 1) // Q_BLOCK)

    def win_mask(kt, dist):
        return (dist >= 0) & (dist < SWA_WINDOW)

    outs = []
    for g in range(SWA_KV_HEADS):
        heads = [g * rep + r for r in range(rep)]
        qs = _stack_heads(q_ref[0], heads).astype(BF16)
        slope_col = _head_const_col(rows, [SWA_SLOPES[h] for h in heads])
        sink_col = _head_const_col(rows, [sink_ref[h] for h in heads])

        q_per_key_tile = KEY_TILE // Q_BLOCK
        m, l, acc = _flash(qs, _key_tile(k_ref, g * HEAD_DIM), _key_tile(v_ref, g * HEAD_DIM),
                           jnp.maximum(qi - n_prev, 0) // q_per_key_tile, qi // q_per_key_tile + 1, t_col,
                           slope_col, win_mask)
        m_all = jnp.maximum(m, sink_col)
        scale = jnp.exp(m - m_all)
        o = acc * scale / (l * scale + jnp.exp(sink_col - m_all))
        outs += [o[r * Q_BLOCK:(r + 1) * Q_BLOCK] for r in range(rep)]
    o_ref[0] = jnp.concatenate(outs, axis=1).astype(o_ref.dtype)


def _swa_attention(proj, sinks):
    b, s, _ = proj.shape
    nq = s // Q_BLOCK
    return pl.pallas_call(
        _swa_body,
        grid=(b, nq),
        in_specs=[
            pl.BlockSpec(memory_space=pltpu.SMEM),
            pl.BlockSpec((1, Q_BLOCK, SWA_WIDTH), lambda i, j: (i, j, COL_SWA_Q // SWA_WIDTH)),
            pl.BlockSpec((1, s, SWA_KV_WIDTH), lambda i, j: (i, 0, COL_SWA_K // SWA_KV_WIDTH)),
            pl.BlockSpec((1, s, SWA_KV_WIDTH), lambda i, j: (i, 0, COL_SWA_V // SWA_KV_WIDTH)),
        ],
        out_specs=pl.BlockSpec((1, Q_BLOCK, SWA_WIDTH), lambda i, j: (i, j, 0)),
        out_shape=jax.ShapeDtypeStruct((b, s, SWA_WIDTH), BF16),
        compiler_params=_cparams(("parallel", "arbitrary")),
        name="swa_attention",
    )(sinks, proj, proj, proj)


GDN_PAIR = 2 * GDN_CHUNK
GDN_STEP_HEADS = 2
GDN_PREP_PAIRS = 4


def _gdn_body(q_ref, k_ref, v_ref, z_ref, ar_ref, br_ref, cwq_ref, cwk_ref, cwv_ref, alog_ref, dtb_ref,
              nw_ref, o_ref, qn_s, kn_s, v_s, u_s, w_s, qd_s, kdt_s, at_s, eg_s, gr_s, br_s):
    s = q_ref.shape[1]
    n_pairs = s // GDN_PAIR
    heads = range(GDN_STEP_HEADS)

    rowi = lax.broadcasted_iota(jnp.int32, (s, 1), 0)

    def conv_silu(x, w):
        y = x * w[GDN_CONV - 1:GDN_CONV, :]
        for j in range(GDN_CONV - 1):
            sh = GDN_CONV - 1 - j
            xs = jnp.where(rowi >= sh, pltpu.roll(x, sh, axis=0), 0.0)
            y = y + xs * w[j:j + 1, :]
        return _silu(y)

    def l2n(t):
        return t * lax.rsqrt(jnp.sum(t * t, axis=-1, keepdims=True) + NORM_EPS)

    for hh in heads:
        cols = slice(hh * HEAD_DIM, (hh + 1) * HEAD_DIM)
        qn_s[hh] = l2n(conv_silu(q_ref[0, :, cols], cwq_ref[:, cols])) * ATTN_SCALE
        kn_s[hh] = l2n(conv_silu(k_ref[0, :, cols], cwk_ref[:, cols]))
        v_s[hh] = conv_silu(v_ref[0, :, cols], cwv_ref[:, cols])
        a_rate = jnp.exp(alog_ref[hh])
        x = ar_ref[0, hh] + dtb_ref[hh]
        gr_s[hh] = -(a_rate * (jnp.maximum(x, 0.0) + jnp.log(1.0 + jnp.exp(-jnp.abs(x)))))
        br_s[hh] = jax.nn.sigmoid(br_ref[0, hh])

    ii = lax.broadcasted_iota(jnp.int32, (GDN_PAIR, GDN_PAIR), 0)
    jj = lax.broadcasted_iota(jnp.int32, (GDN_PAIR, GDN_PAIR), 1)
    same = _div(ii, GDN_CHUNK) == _div(jj, GDN_CHUNK)
    incl = same & (ii >= jj)
    strict = same & (ii > jj)
    diag = ii == jj
    eye = jnp.where(diag, 1.0, 0.0)
    last_of = same & (_mod(jj, GDN_CHUNK) == GDN_CHUNK - 1)
    n_double = int(np.log2(GDN_CHUNK)) - 1

    def tile_rows(p):
        start = p * GDN_PAIR
        return pl.ds(start if isinstance(start, int) else pl.multiple_of(start, GDN_PAIR), GDN_PAIR)

    def prepare_stages(grp):
        chains = [(hh, grp * GDN_PREP_PAIRS + pp) for pp in range(GDN_PREP_PAIRS) for hh in heads]
        rows = [tile_rows(p) for _, p in chains]
        n = range(len(chains))
        beta, gc_col, gl_col, decay = [], [], [], []
        for hh, p in chains:
            g_row = gr_s[hh, pl.ds(p, 1), :]
            beta.append(jnp.sum(jnp.where(diag, br_s[hh, pl.ds(p, 1), :], 0.0), axis=1, keepdims=True))
            gcc = jnp.sum(jnp.where(incl, g_row, 0.0), axis=1, keepdims=True)
            gcr = jnp.sum(jnp.where(diag, gcc, 0.0), axis=0, keepdims=True)
            gc_col.append(gcc)
            gl_col.append(jnp.sum(jnp.where(last_of, gcr, 0.0), axis=1, keepdims=True))
            decay.append(jnp.exp(jnp.where(incl, gcc - gcr, NEG_INF)))
        q = [qn_s[hh, rows[c], :] for c, (hh, _) in enumerate(chains)]
        k = [kn_s[hh, rows[c], :] for c, (hh, _) in enumerate(chains)]
        v = [v_s[hh, rows[c], :] for c, (hh, _) in enumerate(chains)]
        kb = [k[c] * beta[c] for c in n]
        kf = [k[c].astype(BF16) for c in n]
        kk = [_dot_nt(kb[c].astype(BF16), kf[c]) for c in n]
        qk = [_dot_nt(q[c].astype(BF16), kf[c]) for c in n]
        yield
        pw = [-jnp.where(strict, kk[c] * decay[c], 0.0) for c in n]
        tinv = [eye + pw[c] for c in n]
        for _ in range(n_double):
            pw = [_dot_hi(pw[c], pw[c]) for c in n]
            yield
            tinv = [tinv[c] + _dot_hi(tinv[c], pw[c]) for c in n]
            yield
        egc = [jnp.exp(gc_col[c]) for c in n]
        sol = [_dot_hi(tinv[c], jnp.concatenate([v[c] * beta[c], kb[c] * egc[c]], axis=1)) for c in n]
        yield
        for c, (hh, _) in enumerate(chains):
            u_s[hh, rows[c], :] = sol[c][:, :HEAD_DIM]
            w_s[hh, rows[c], :] = sol[c][:, HEAD_DIM:]
            at_s[hh, rows[c], :] = jnp.where(incl, qk[c] * decay[c], 0.0)
            qd_s[hh, rows[c], :] = q[c] * egc[c]
            kdt_s[hh, rows[c], :] = (k[c] * jnp.exp(gl_col[c] - gc_col[c])).T
            eg_s[hh, rows[c], :] = jnp.broadcast_to(jnp.exp(gl_col[c]), (GDN_PAIR, HEAD_DIM))

    rin = lax.broadcasted_iota(jnp.int32, (GDN_PAIR, 1), 0)
    nw = nw_ref[...]

    def scan_stages(grp, states):
        for pp in range(GDN_PREP_PAIRS):
            rows = tile_rows(grp * GDN_PREP_PAIRS + pp)
            u = [u_s[hh, rows, :] for hh in heads]
            w = [w_s[hh, rows, :].astype(BF16) for hh in heads]
            qd = [qd_s[hh, rows, :].astype(BF16) for hh in heads]
            at = [at_s[hh, rows, :].astype(BF16) for hh in heads]
            kdt = [kdt_s[hh, rows, :].astype(BF16) for hh in heads]
            eg = [eg_s[hh, rows, :] for hh in heads]
            o = [jnp.zeros((GDN_PAIR, HEAD_DIM), F32) for _ in heads]
            for c in range(GDN_PAIR // GDN_CHUNK):
                in_c = _div(rin, GDN_CHUNK) == c
                sb = [states[hh].astype(BF16) for hh in heads]
                ws = [_dot(w[hh], sb[hh]) for hh in heads]
                qs = [_dot(qd[hh], sb[hh]) for hh in heads]
                yield
                vb = [jnp.where(in_c, u[hh] - ws[hh], 0.0).astype(BF16) for hh in heads]
                av = [_dot(at[hh], vb[hh]) for hh in heads]
                kv = [_dot(kdt[hh], vb[hh]) for hh in heads]
                yield
                o = [jnp.where(in_c, qs[hh] + av[hh], o[hh]) for hh in heads]
                for hh in heads:
                    states[hh] = states[hh] * eg[hh][c * GDN_CHUNK:c * GDN_CHUNK + 1, :] + kv[hh]
            for hh in heads:
                cols = slice(hh * HEAD_DIM, (hh + 1) * HEAD_DIM)
                on = o[hh] * lax.rsqrt(jnp.mean(o[hh] * o[hh], axis=-1, keepdims=True) + NORM_EPS) * nw
                o_ref[0, rows, cols] = (on * _silu(z_ref[0, rows, cols])).astype(o_ref.dtype)

    def alternate(*phases):
        live = list(phases)
        while live:
            for ph in list(live):
                if next(ph, live) is live:
                    live.remove(ph)

    n_groups = n_pairs // GDN_PREP_PAIRS
    alternate(prepare_stages(0))

    def group_step(grp, states):
        states = list(states)
        alternate(prepare_stages(grp), scan_stages(grp - 1, states))
        return tuple(states)

    states = lax.fori_loop(1, n_groups, group_step, tuple(jnp.zeros((HEAD_DIM, HEAD_DIM), F32) for _ in heads))
    alternate(scan_stages(n_groups - 1, list(states)))


def _gdn_mixer(proj, conv_w, a_log, dt_bias, norm_w):
    b, s, _ = proj.shape
    h = GDN_HEADS
    nh = GDN_STEP_HEADS
    n_pairs = s // GDN_PAIR
    wid = nh * HEAD_DIM

    def rows_of(col):
        return proj[:, :, col:col + h].transpose(0, 2, 1).reshape(b, h, n_pairs, GDN_PAIR)

    seq = lambda off: pl.BlockSpec((1, s, wid), lambda i, j: (i, 0, (COL_GDN + off * GDN_WIDTH) // wid + j))
    gate = pl.BlockSpec((1, nh, n_pairs, GDN_PAIR), lambda i, j: (i, j, 0, 0))
    cw = lambda off: pl.BlockSpec((GDN_CONV, wid), lambda i, j: (0, off * (h // nh) + j))
    scalar = pl.BlockSpec((nh, 1, 1), lambda i, j: (j, 0, 0))
    seq_f32 = pltpu.VMEM((nh, s, HEAD_DIM), F32)
    row_f32 = pltpu.VMEM((nh, n_pairs, GDN_PAIR), F32)
    return pl.pallas_call(
        _gdn_body,
        grid=(b, h // nh),
        in_specs=[
            seq(0), seq(1), seq(2), seq(3),
            gate, gate,
            cw(0), cw(1), cw(2),
            scalar, scalar,
            pl.BlockSpec((1, HEAD_DIM), lambda i, j: (0, 0)),
        ],
        out_specs=pl.BlockSpec((1, s, wid), lambda i, j: (i, 0, j)),
        out_shape=jax.ShapeDtypeStruct((b, s, GDN_WIDTH), BF16),
        scratch_shapes=[seq_f32] * 9 + [row_f32, row_f32],
        compiler_params=_cparams(("parallel", "arbitrary")),
        name="gdn_mixer",
    )(proj, proj, proj, proj, rows_of(COL_SMALL + SMALL_A), rows_of(COL_SMALL + SMALL_B), conv_w, conv_w, conv_w,
      a_log.reshape(h, 1, 1), dt_bias.reshape(h, 1, 1), norm_w.reshape(1, HEAD_DIM))


def _outproj_body(on_ref, og_ref, os_ref, x_ref, w_ref, o_ref):
    mix = jnp.concatenate([on_ref[...], og_ref[...], os_ref[...]], axis=1)
    o_ref[...] = x_ref[...] + _dot(mix, w_ref[...])


def _outproj_router_body(on_ref, og_ref, os_ref, x_ref, w_ref, nw_ref, r_ref, o_ref, h_ref, lg_ref):
    mix = jnp.concatenate([on_ref[...], og_ref[...], os_ref[...]], axis=1)
    x1 = x_ref[...] + _dot(mix, w_ref[...])
    o_ref[...] = x1
    h = _rms(x1, nw_ref[...])
    h_ref[...] = h
    lg_ref[...] = _dot_hi(h, r_ref[...])


def _out_proj(o_nsa, o_gdn, o_swa, x2d, w_bf16, ffn_norm=None, router=None, tm=512):
    n, d = x2d.shape
    row = lambda wid: pl.BlockSpec((tm, wid), lambda i: (i, 0))
    full = lambda a, c: pl.BlockSpec((a, c), lambda i: (0, 0))
    in_specs = [row(NSA_WIDTH), row(GDN_WIDTH), row(SWA_WIDTH), row(d), full(d, d)]
    args = [o_nsa, o_gdn, o_swa, x2d, w_bf16]
    if router is None:
        return pl.pallas_call(
            _outproj_body, grid=(n // tm,), in_specs=in_specs, out_specs=row(d),
            out_shape=jax.ShapeDtypeStruct((n, d), F32),
            compiler_params=_cparams(("parallel",)), name="out_proj",
        )(*args)
    router_pad = jnp.pad(router, ((0, 0), (0, LANES - router.shape[1])))
    return pl.pallas_call(
        _outproj_router_body, grid=(n // tm,),
        in_specs=in_specs + [full(1, d), full(d, LANES)],
        out_specs=[row(d), row(d), row(LANES)],
        out_shape=[jax.ShapeDtypeStruct((n, d), F32), jax.ShapeDtypeStruct((n, d), F32),
                   jax.ShapeDtypeStruct((n, LANES), F32)],
        compiler_params=_cparams(("parallel",)), name="out_proj_router",
    )(*args, ffn_norm.reshape(1, d), router_pad)


def _swiglu_tile(h, wg, wu, wd):
    a = _dot(h, wg.astype(BF16))
    b = _dot(h, wu.astype(BF16))
    return _dot((_silu(a) * b).astype(BF16), wd.astype(BF16))


def _dense_body(x_ref, nw_ref, wg_ref, wu_ref, wd_ref, o_ref, h_ref):
    @pl.when(pl.program_id(1) == 0)
    def _():
        x = x_ref[...]
        h_ref[...] = _rms(x, nw_ref[...]).astype(BF16)
        o_ref[...] = x

    o_ref[...] += _swiglu_tile(h_ref[...], wg_ref[...], wu_ref[...], wd_ref[...])


def _dense_ffn(x2d, norm_w, wg, wu, wd, tm=1024, tf=256):
    n, d = x2d.shape
    ff = wg.shape[1]
    return pl.pallas_call(
        _dense_body,
        grid=(n // tm, ff // tf),
        in_specs=[
            pl.BlockSpec((tm, d), lambda i, f: (i, 0), pipeline_mode=pl.Buffered(1)),
            pl.BlockSpec((1, d), lambda i, f: (0, 0)),
            pl.BlockSpec((d, tf), lambda i, f: (0, f)),
            pl.BlockSpec((d, tf), lambda i, f: (0, f)),
            pl.BlockSpec((tf, d), lambda i, f: (f, 0)),
        ],
        out_specs=pl.BlockSpec((tm, d), lambda i, f: (i, 0)),
        out_shape=jax.ShapeDtypeStruct((n, d), F32),
        scratch_shapes=[pltpu.VMEM((tm, d), BF16)],
        compiler_params=_cparams(("parallel", "arbitrary")),
        name="dense_ffn",
    )(x2d, norm_w.reshape(1, d), wg, wu, wd)


def _row_gather(idx_ref, base, count, src_hbm, dst, sem):
    def body(r, carry):
        t = idx_ref[base + r]
        pltpu.make_async_copy(src_hbm.at[pl.ds(t, 1), :], dst.at[pl.ds(r, 1), :], sem).start()
        return carry

    lax.fori_loop(0, count, body, 0, unroll=8)


def _row_gather_wait(count, src_hbm, dst, sem):
    pltpu.make_async_copy(src_hbm.at[pl.ds(0, count), :], dst, sem).wait()


MOE_STEP_ROWS = 32


def _moe_body(be_ref, nu_ref, br_ref, tok_ref, h_hbm, wg_ref, wu_ref, wd_ref, o_ref, xbuf, x_s, sem, *, nf):
    i = pl.program_id(0)
    f = pl.program_id(1)
    tm = x_s.shape[0]
    n_used = nu_ref[0]
    used = i < n_used
    more = i + 1 < n_used
    small = br_ref[i] <= tm // 2
    head_rows = tm - MOE_STEP_ROWS * nf
    assert head_rows >= 0

    @pl.when(f == 0)
    def _():
        o_ref[...] = jnp.zeros_like(o_ref)

        @pl.when(i == 0)
        def _():
            _row_gather(tok_ref, 0, tm, h_hbm, xbuf, sem)

        @pl.when(used)
        def _():
            _row_gather_wait(tm, h_hbm, xbuf, sem)
            x_s[...] = xbuf[...].astype(BF16)

        @pl.when(more)
        def _():
            _row_gather(tok_ref, (i + 1) * tm, head_rows, h_hbm, xbuf, sem)

    def accumulate(rows):
        o_ref[0:rows, :] += _swiglu_tile(x_s[0:rows, :], wg_ref[0], wu_ref[0], wd_ref[0])

    def fetch_next_rows():
        base = head_rows + f * MOE_STEP_ROWS
        for u in range(MOE_STEP_ROWS):
            t = tok_ref[(i + 1) * tm + base + u]
            pltpu.make_async_copy(h_hbm.at[pl.ds(t, 1), :], xbuf.at[pl.ds(base + u, 1), :], sem).start()

    for rows, fits in ((tm // 2, small), (tm, jnp.logical_not(small))):
        @pl.when(more & fits)
        def _(rows=rows):
            fetch_next_rows()
            accumulate(rows)

        @pl.when(used & jnp.logical_not(more) & fits)
        def _(rows=rows):
            accumulate(rows)


def _moe_ffn(h, row_tok, blk_expert, n_used, blk_rows, wg, wu, wd, tm, tf=256):
    p = row_tok.shape[0]
    d = h.shape[1]
    ff = wg.shape[2]
    nb = p // tm
    nf = ff // tf

    def fsel(i, f, nu):
        return jnp.where(i < nu[0], f, nf - 1)

    grid_spec = pltpu.PrefetchScalarGridSpec(
        num_scalar_prefetch=4,
        grid=(nb, nf),
        in_specs=[
            pl.BlockSpec(memory_space=pl.ANY),
            pl.BlockSpec((1, d, tf), lambda i, f, be, nu, br, tok: (be[i], 0, fsel(i, f, nu))),
            pl.BlockSpec((1, d, tf), lambda i, f, be, nu, br, tok: (be[i], 0, fsel(i, f, nu))),
            pl.BlockSpec((1, tf, d), lambda i, f, be, nu, br, tok: (be[i], fsel(i, f, nu), 0)),
        ],
        out_specs=pl.BlockSpec((tm, d), lambda i, f, be, nu, br, tok: (i, 0)),
        scratch_shapes=[pltpu.VMEM((tm, d), F32), pltpu.VMEM((tm, d), BF16), pltpu.SemaphoreType.DMA(())],
    )
    return pl.pallas_call(
        functools.partial(_moe_body, nf=nf),
        grid_spec=grid_spec,
        out_shape=jax.ShapeDtypeStruct((p, d), F32),
        compiler_params=_cparams(("arbitrary", "arbitrary")),
        name="moe_ffn",
    )(blk_expert, n_used, blk_rows, row_tok, h, wg, wu, wd)


def _moe_layer(h, logits, wg, wu, wd, tm=1024):
    n = h.shape[0]
    top_logit, top_idx = lax.top_k(logits, TOP_K)
    gate = jax.nn.softmax(top_logit, axis=-1)
    a = n * TOP_K
    e_flat = top_idx.reshape(a)
    onehot = (e_flat[:, None] == jnp.arange(N_EXPERTS, dtype=e_flat.dtype)[None]).astype(jnp.int32)
    rank = jnp.sum((jnp.cumsum(onehot, axis=0) - onehot) * onehot, axis=1)
    counts = jnp.sum(onehot, axis=0)
    padded = (counts + tm - 1) // tm * tm
    pad_end = jnp.cumsum(padded)
    pad_start = pad_end - padded
    dest = (pad_start[e_flat] + rank).astype(jnp.int32)
    nb = a // tm + N_EXPERTS
    n_used = (pad_end[-1] // tm).astype(jnp.int32)
    blk = jnp.minimum(jnp.arange(nb, dtype=jnp.int32), n_used - 1)
    blk_expert = jnp.sum((blk[:, None] * tm >= pad_end[None, :]).astype(jnp.int32), axis=1)
    blk_expert = jnp.minimum(blk_expert, N_EXPERTS - 1)
    blk_rows = jnp.clip(counts[blk_expert] - (blk * tm - pad_start[blk_expert]), 0, tm).astype(jnp.int32)
    row_tok = jnp.zeros((nb * tm,), jnp.int32).at[dest].set(jnp.arange(a, dtype=jnp.int32) // TOP_K)
    ys = _moe_ffn(h, row_tok, blk_expert, n_used.reshape(1), blk_rows, wg, wu, wd, tm)
    return ys, dest, gate


def _combine_body(d_ref, x_ref, g_ref, w_ref, ys_hbm, o_ref, buf, sem, *, final_norm):
    j = pl.program_id(0)
    tt = x_ref.shape[0]
    slot = j % 2

    def start(step, sl):
        for k in range(TOP_K):
            _row_gather(d_ref, (step * TOP_K + k) * tt, tt, ys_hbm, buf.at[sl, k], sem.at[sl])

    @pl.when(j == 0)
    def _():
        start(0, 0)

    for k in range(TOP_K):
        _row_gather_wait(tt, ys_hbm, buf.at[slot, k], sem.at[slot])

    @pl.when(j + 1 < pl.num_programs(0))
    def _():
        start(j + 1, 1 - slot)

    g = g_ref[...]
    f = buf[slot, 0] * g[:, 0:1]
    for k in range(1, TOP_K):
        f = f + buf[slot, k] * g[:, k:k + 1]
    y = x_ref[...] + f
    o_ref[...] = _rms(y, w_ref[...]) if final_norm else y


def _moe_combine(x2d, ys, dest, gate, norm_w, final_norm, tt=256):
    n, d = x2d.shape
    steps = n // tt
    order = dest.reshape(steps, tt, TOP_K).transpose(0, 2, 1).reshape(n * TOP_K)
    grid_spec = pltpu.PrefetchScalarGridSpec(
        num_scalar_prefetch=1,
        grid=(steps,),
        in_specs=[
            pl.BlockSpec((tt, d), lambda j, dr: (j, 0)),
            pl.BlockSpec((tt, TOP_K), lambda j, dr: (j, 0)),
            pl.BlockSpec((1, d), lambda j, dr: (0, 0)),
            pl.BlockSpec(memory_space=pl.ANY),
        ],
        out_specs=pl.BlockSpec((tt, d), lambda j, dr: (j, 0)),
        scratch_shapes=[pltpu.VMEM((2, TOP_K, tt, d), F32), pltpu.SemaphoreType.DMA((2,))],
    )
    return pl.pallas_call(
        functools.partial(_combine_body, final_norm=final_norm),
        grid_spec=grid_spec,
        out_shape=jax.ShapeDtypeStruct((n, d), F32),
        compiler_params=_cparams(("arbitrary",)),
        name="moe_combine",
    )(order, x2d, gate, norm_w.reshape(1, d), ys)


def kernel(x, attn_norm, w_in, cmp_pos_k, cmp_pos_v, cmp_w1_k, cmp_w2_k, cmp_w1_v, cmp_w2_v, gdn_conv_w, gdn_a_log, gdn_dt_bias, gdn_norm_w, swa_sinks, w_out, ffn_norm, dense_w_gate, dense_w_up, dense_w_down, moe_router, moe_w_gate, moe_w_up, moe_w_down, final_norm):
    b, s, d = x.shape
    n = b * s
    depth = w_in.shape[0]
    assert depth % 2 == 0, "the trunk ends on an expert layer, whose combine feeds the final norm"
    x2d = x.reshape(n, d)
    w_in_cols = _reorder_w_in(w_in)
    for layer in range(depth):
        proj = _in_proj(x2d, attn_norm[layer], w_in_cols, layer).reshape(b, s, PROJ_COLS)
        n_strides = s // CMP_STRIDE
        ck = proj[:, :, COL_NSA_KV:COL_NSA_KV + HEAD_DIM].reshape(b, n_strides, CMP_STRIDE * HEAD_DIM)
        cv = proj[:, :, COL_NSA_KV + HEAD_DIM:COL_NSA_KV + 2 * HEAD_DIM].reshape(b, n_strides, CMP_STRIDE * HEAD_DIM)
        kc, vc = _nsa_compress(ck, cv, cmp_pos_k[layer], cmp_pos_v[layer], cmp_w1_k[layer], cmp_w2_k[layer],
                               cmp_w1_v[layer], cmp_w2_v[layer])
        o_nsa = _nsa_attention(proj, kc, vc).reshape(n, NSA_WIDTH)
        o_gdn = _gdn_mixer(proj, gdn_conv_w[layer], gdn_a_log[layer], gdn_dt_bias[layer],
                           gdn_norm_w[layer]).reshape(n, GDN_WIDTH)
        o_swa = _swa_attention(proj, swa_sinks[layer]).reshape(n, SWA_WIDTH)
        w_o = w_out[layer].astype(BF16)
        i = layer // 2
        if layer % 2 == 0:
            x1 = _out_proj(o_nsa, o_gdn, o_swa, x2d, w_o)
            x2d = _dense_ffn(x1, ffn_norm[layer], dense_w_gate[i], dense_w_up[i], dense_w_down[i])
        else:
            x1, h2, logits = _out_proj(o_nsa, o_gdn, o_swa, x2d, w_o, ffn_norm[layer], moe_router[i])
            ys, dest, gate = _moe_layer(h2, logits[:, :N_EXPERTS], moe_w_gate[i], moe_w_up[i], moe_w_down[i])
            x2d = _moe_combine(x1, ys, dest, gate, final_norm, final_norm=layer == depth - 1)
    return x2d.reshape(b, s, d)
```

```python
import functools

import jax
import jax.numpy as jnp
import numpy as np
from jax import lax
from jax.experimental import pallas as pl
from jax.experimental.pallas import tpu as pltpu

F32 = jnp.float32
BF16 = jnp.bfloat16

D_MODEL = 2048
HEAD_DIM = 128
NSA_HEADS = 4
SWA_HEADS = 4
GDN_HEADS = 8
SWA_KV_HEADS = 2
NSA_WIDTH = NSA_HEADS * HEAD_DIM
GDN_WIDTH = GDN_HEADS * HEAD_DIM
SWA_WIDTH = SWA_HEADS * HEAD_DIM
SWA_KV_WIDTH = SWA_KV_HEADS * HEAD_DIM

CMP_BLOCK = 32
CMP_STRIDE = 16
SEL_BLOCK = 64
SEL_TOPN = 8
NSA_WINDOW = 512
FORCE_BONUS = 1000.0
GDN_CONV = 4
GDN_CHUNK = 64
SWA_WINDOW = 128
Q_BLOCK = 128
N_EXPERTS = 8
TOP_K = 2
NORM_EPS = 1e-6
NEG_INF = -1e30
ATTN_SCALE = HEAD_DIM ** -0.5

LANES = 128
VMEM_LIMIT_BYTES = 56 * 1024 * 1024

COL_NSA_Q = 0
COL_SWA_Q = 512
COL_GDN = 1024
COL_NSA_KV = COL_GDN + 4 * GDN_WIDTH
COL_SWA_K = COL_NSA_KV + 6 * HEAD_DIM
COL_SWA_V = COL_SWA_K + SWA_KV_WIDTH
COL_SMALL = COL_SWA_V + SWA_KV_WIDTH
PROJ_COLS = 6656
PROJ_TN = 1664
SMALL_A = 3 * NSA_HEADS
SMALL_B = SMALL_A + GDN_HEADS


def _alibi_slopes():
    n = NSA_HEADS + SWA_HEADS
    s = [2.0 ** (-8.0 * i / n) for i in range(1, n + 1)]
    return tuple(s[SWA_HEADS:]), tuple(s[:SWA_HEADS])


NSA_SLOPES, SWA_SLOPES = _alibi_slopes()


def _div(v, c):
    assert c & (c - 1) == 0
    return v >> (c.bit_length() - 1)


def _mod(v, c):
    assert c & (c - 1) == 0
    return v & (c - 1)


def _cparams(sem):
    return pltpu.CompilerParams(dimension_semantics=sem, vmem_limit_bytes=VMEM_LIMIT_BYTES)


def _rms(x, w):
    return x * lax.rsqrt(jnp.mean(x * x, axis=-1, keepdims=True) + NORM_EPS) * w


def _silu(x):
    return x * jax.nn.sigmoid(x)


def _dot(a, b):
    return jnp.dot(a, b, preferred_element_type=F32)


def _dot_nt(a, b):
    return lax.dot_general(a, b, (((1,), (1,)), ((), ())), preferred_element_type=F32)


def _split2(a):
    hi = a.astype(BF16)
    lo = (a - hi.astype(F32)).astype(BF16)
    return hi, lo


def _dot_hi(a, b):
    ah, al = _split2(a)
    bh, bl = _split2(b)
    return _dot(ah, bh) + (_dot(ah, bl) + _dot(al, bh))


def _dot_exact_rhs(a, b_bf16):
    a0 = a.astype(BF16)
    r = a - a0.astype(F32)
    a1 = r.astype(BF16)
    a2 = (r - a1.astype(F32)).astype(BF16)
    return _dot(a0, b_bf16) + (_dot(a1, b_bf16) + _dot(a2, b_bf16))


def _inproj_body(x_ref, nw_ref, w_ref, o_ref, h_ref):
    @pl.when(pl.program_id(1) == 0)
    def _():
        h_ref[...] = _rms(x_ref[...], nw_ref[...]).astype(BF16)

    o_ref[...] = _dot(h_ref[...], w_ref[...])


def _in_proj(x2d, norm_w, w_all, layer, tm=1024):
    n, d = x2d.shape
    c = w_all.shape[2]
    return pl.pallas_call(
        _inproj_body,
        grid=(n // tm, c // PROJ_TN),
        in_specs=[
            pl.BlockSpec((tm, d), lambda i, j: (i, 0)),
            pl.BlockSpec((1, d), lambda i, j: (0, 0)),
            pl.BlockSpec((None, d, PROJ_TN), lambda i, j: (layer, 0, j)),
        ],
        out_specs=pl.BlockSpec((tm, PROJ_TN), lambda i, j: (i, j)),
        out_shape=jax.ShapeDtypeStruct((n, c), F32),
        scratch_shapes=[pltpu.VMEM((tm, d), BF16)],
        compiler_params=_cparams(("parallel", "arbitrary")),
        name="in_proj",
    )(x2d, norm_w.reshape(1, d), w_all)


def _w_in_segments():
    src = {}
    o = 0
    for name, size in (("nsa_q", NSA_WIDTH), ("nsa_kv", 6 * HEAD_DIM), ("nsa_g", 3 * NSA_HEADS),
                       ("gdn", 4 * GDN_WIDTH), ("ga", GDN_HEADS), ("gb", GDN_HEADS),
                       ("swa_q", SWA_WIDTH), ("swa_k", SWA_KV_WIDTH), ("swa_v", SWA_KV_WIDTH)):
        src[name] = (o, size)
        o += size
    dst = {"nsa_q": COL_NSA_Q, "swa_q": COL_SWA_Q, "gdn": COL_GDN, "nsa_kv": COL_NSA_KV, "swa_k": COL_SWA_K,
           "swa_v": COL_SWA_V, "nsa_g": COL_SMALL, "ga": COL_SMALL + SMALL_A, "gb": COL_SMALL + SMALL_B}
    return [(src[k][0], dst[k], src[k][1]) for k in src], o


def _reorder_body(start_ref, wt_hbm, small_ref, o_ref, buf, sem, *, n_main):
    j = pl.program_id(0)
    slot = j % 2
    layers = o_ref.shape[0]

    def fetch(blk, sl):
        return pltpu.make_async_copy(wt_hbm.at[pl.ds(start_ref[blk], LANES)], buf.at[sl], sem.at[sl])

    @pl.when(j == 0)
    def _():
        fetch(0, 0).start()

    @pl.when(j < n_main)
    def _():
        fetch(j, slot).wait()

    @pl.when(j + 1 < n_main)
    def _():
        fetch(j + 1, 1 - slot).start()

    @pl.when(j < n_main)
    def _():
        for l in range(layers):
            o_ref[l] = buf[slot, :, l, :].T.astype(o_ref.dtype)

    @pl.when(j == n_main)
    def _():
        for l in range(layers):
            o_ref[l] = small_ref[:, l, :].T.astype(o_ref.dtype)

    @pl.when(j > n_main)
    def _():
        o_ref[...] = jnp.zeros_like(o_ref)


def _reorder_w_in(w_all):
    layers, d, c = w_all.shape
    segs, total = _w_in_segments()
    assert total == c
    wt = jnp.transpose(w_all, (2, 0, 1))
    n_main = COL_SMALL // LANES
    starts, small, small_at = [None] * n_main, [], COL_SMALL
    for s0, d0, wid in segs:
        if wid % LANES == 0:
            assert d0 % LANES == 0
            for i in range(wid // LANES):
                starts[d0 // LANES + i] = s0 + i * LANES
        else:
            assert d0 == small_at, "the narrow segments fill the last block back to back"
            small.append(wt[s0:s0 + wid])
            small_at += wid
    assert None not in starts
    small = jnp.concatenate(small + [jnp.zeros((COL_SMALL + LANES - small_at, layers, d), wt.dtype)], axis=0)
    grid_spec = pltpu.PrefetchScalarGridSpec(
        num_scalar_prefetch=1,
        grid=(PROJ_COLS // LANES,),
        in_specs=[pl.BlockSpec(memory_space=pl.ANY),
                  pl.BlockSpec((LANES, layers, d), lambda j, st: (0, 0, 0))],
        out_specs=pl.BlockSpec((layers, d, LANES), lambda j, st: (0, 0, j)),
        scratch_shapes=[pltpu.VMEM((2, LANES, layers, d), F32), pltpu.SemaphoreType.DMA((2,))],
    )
    return pl.pallas_call(
        functools.partial(_reorder_body, n_main=n_main),
        grid_spec=grid_spec,
        out_shape=jax.ShapeDtypeStruct((layers, d, PROJ_COLS), BF16),
        compiler_params=_cparams(("arbitrary",)),
        name="reorder_w_in",
    )(jnp.asarray(starts, jnp.int32), wt, small)


KEY_TILE = 256


def _key_tile(ref, col0=0):
    return lambda kt: ref[0, pl.ds(pl.multiple_of(kt * KEY_TILE, KEY_TILE), KEY_TILE), col0:col0 + HEAD_DIM]


def _flash(qs, get_k, get_v, lo, hi, t_col, slope_col, mask_fn):
    r = qs.shape[0]
    lane = lax.broadcasted_iota(jnp.int32, (r, KEY_TILE), 1)

    def body(kt, carry):
        m, l, acc = carry
        k = get_k(kt).astype(BF16)
        v = get_v(kt).astype(BF16)
        s = _dot_nt(qs, k) * ATTN_SCALE
        dist = t_col - (kt * KEY_TILE + lane)
        mask = mask_fn(kt, dist)
        logits = jnp.where(mask, s - slope_col * dist.astype(F32), NEG_INF)
        m_new = jnp.maximum(m, jnp.max(logits, axis=-1, keepdims=True))
        alpha = jnp.exp(m - m_new)
        p = jnp.where(mask, jnp.exp(logits - m_new), 0.0)
        l = alpha * l + jnp.sum(p, axis=-1, keepdims=True)
        acc = alpha * acc + _dot(p.astype(BF16), v)
        return m_new, l, acc

    init = (jnp.full((r, 1), NEG_INF, F32), jnp.zeros((r, 1), F32), jnp.zeros((r, HEAD_DIM), F32))
    return lax.fori_loop(lo, hi, body, init)


def _stack_heads(x, heads):
    return jnp.concatenate([x[:, h * HEAD_DIM:(h + 1) * HEAD_DIM] for h in heads], axis=0)


def _head_const_col(rows, values):
    head = _div(lax.broadcasted_iota(jnp.int32, (rows, 1), 0), Q_BLOCK)
    col = jnp.full((rows, 1), values[-1], F32)
    for h in range(len(values) - 2, -1, -1):
        col = jnp.where(head == h, values[h], col)
    return col


def _compress_body(ck_ref, cv_ref, pk_ref, pv_ref, w1k_ref, w2k_ref, w1v_ref, w2v_ref, kc_ref, vc_ref):
    half = CMP_STRIDE * HEAD_DIM

    def run(c_ref, p_ref, w1_ref, w2_ref, o_ref):
        c = c_ref[0]
        lo = _dot((c + p_ref[0:1, :]).astype(BF16), w1_ref[0:half, :].astype(BF16))
        hi = _dot((c + p_ref[1:2, :]).astype(BF16), w1_ref[half:2 * half, :].astype(BF16))
        hid = lo + pltpu.roll(hi, hi.shape[0] - 1, axis=0)
        o_ref[0] = _dot(_silu(hid).astype(BF16), w2_ref[...].astype(BF16))

    run(ck_ref, pk_ref, w1k_ref, w2k_ref, kc_ref)
    run(cv_ref, pv_ref, w1v_ref, w2v_ref, vc_ref)


def _nsa_compress(ck, cv, pos_k, pos_v, w1k, w2k, w1v, w2v):
    b, ns, wid = ck.shape
    hid = w1k.shape[1]
    full = lambda shape: pl.BlockSpec(shape, lambda i: (0,) * len(shape))
    return pl.pallas_call(
        _compress_body,
        grid=(b,),
        in_specs=[
            pl.BlockSpec((1, ns, wid), lambda i: (i, 0, 0)),
            pl.BlockSpec((1, ns, wid), lambda i: (i, 0, 0)),
            full((2, wid)), full((2, wid)),
            full((2 * wid, hid)), full((hid, HEAD_DIM)),
            full((2 * wid, hid)), full((hid, HEAD_DIM)),
        ],
        out_specs=[pl.BlockSpec((1, ns, HEAD_DIM), lambda i: (i, 0, 0))] * 2,
        out_shape=[jax.ShapeDtypeStruct((b, ns, HEAD_DIM), F32)] * 2,
        compiler_params=_cparams(("parallel",)),
        name="nsa_compress",
    )(ck, cv, pos_k.reshape(2, wid), pos_v.reshape(2, wid), w1k, w2k, w1v, w2v)


def _nsa_body(q_ref, gl_ref, kc_ref, vc_ref, ksl_ref, vsl_ref, kw_ref, vw_ref, o_ref, *, n_sel):
    qi = pl.program_id(1)
    h4 = range(NSA_HEADS)
    rows = NSA_HEADS * Q_BLOCK
    qs = _stack_heads(q_ref[0], h4).astype(BF16)
    row = lax.broadcasted_iota(jnp.int32, (rows, 1), 0)
    t_col = qi * Q_BLOCK + _mod(row, Q_BLOCK)
    slope_col = _head_const_col(rows, NSA_SLOPES)

    lane = lax.broadcasted_iota(jnp.int32, (rows, LANES), 1)
    c_dist = t_col - (lane * CMP_STRIDE + CMP_BLOCK - 1)
    c_mask = c_dist >= 0
    sc = _dot_nt(qs, kc_ref[0].astype(BF16)) * ATTN_SCALE
    logits = jnp.where(c_mask, sc - slope_col * c_dist.astype(F32), NEG_INF)
    e = jnp.exp(logits - jnp.max(logits, axis=-1, keepdims=True))
    p = e / jnp.sum(e, axis=-1, keepdims=True)
    p = p * (t_col >= CMP_BLOCK - 1).astype(F32)
    o_cmp = _dot(p.astype(BF16), vc_ref[0].astype(BF16))

    psum = p[0:Q_BLOCK] + p[Q_BLOCK:2 * Q_BLOCK] + p[2 * Q_BLOCK:3 * Q_BLOCK] + p[3 * Q_BLOCK:4 * Q_BLOCK]
    cn = lax.broadcasted_iota(jnp.int32, (LANES, LANES), 0) * CMP_STRIDE
    sj = lax.broadcasted_iota(jnp.int32, (LANES, LANES), 1) * SEL_BLOCK
    ov = jnp.maximum(jnp.minimum(cn + CMP_BLOCK, sj + SEL_BLOCK) - jnp.maximum(cn, sj), 0)
    ov = (ov.astype(F32) * (1.0 / CMP_BLOCK)).astype(BF16)
    imp = _dot_exact_rhs(psum, ov)
    blk = lax.broadcasted_iota(jnp.int32, (Q_BLOCK, LANES), 1)
    tq = qi * Q_BLOCK + lax.broadcasted_iota(jnp.int32, (Q_BLOCK, 1), 0)
    cur = _div(tq, SEL_BLOCK)
    forced = (blk == 0) | (blk == cur) | (blk == cur - 1)
    score = jnp.where(blk <= cur, imp + jnp.where(forced, FORCE_BONUS, 0.0), -1.0)
    score_t = score.T[0:n_sel, :]
    blk_t = lax.broadcasted_iota(jnp.int32, (n_sel, Q_BLOCK), 0)
    rank = jnp.zeros((n_sel, Q_BLOCK), F32)
    for i in range(n_sel):
        s_i = score_t[i:i + 1, :]
        ahead = (s_i > score_t) | ((s_i == score_t) & (blk_t > i))
        rank = rank + jnp.where(ahead, 1.0, 0.0)
    sel_t = jnp.where(rank < float(min(SEL_TOPN, n_sel)), 1.0, 0.0)
    sel_b = jnp.concatenate([sel_t, jnp.zeros((LANES - n_sel, Q_BLOCK), F32)], axis=0).T.astype(BF16)

    ej = lax.broadcasted_iota(jnp.int32, (LANES, KEY_TILE), 0)
    ec = _div(lax.broadcasted_iota(jnp.int32, (LANES, KEY_TILE), 1), SEL_BLOCK)
    per_tile = KEY_TILE // SEL_BLOCK
    q_per_key_tile = KEY_TILE // Q_BLOCK
    last_tile = qi // q_per_key_tile

    def sel_mask(kt, dist):
        expand = jnp.where(ej == kt * per_tile + ec, 1.0, 0.0).astype(BF16)
        member = _dot(sel_b, expand)
        member = jnp.concatenate([member] * NSA_HEADS, axis=0)
        return (member > 0.5) & (dist >= 0)

    m, l, acc = _flash(qs, _key_tile(ksl_ref), _key_tile(vsl_ref), 0, last_tile + 1, t_col, slope_col, sel_mask)
    o_slc = acc / l

    def win_mask(kt, dist):
        return (dist >= 0) & (dist < NSA_WINDOW)

    n_prev = -(-(NSA_WINDOW - 1) // Q_BLOCK)
    first_tile = jnp.maximum(qi - n_prev, 0) // q_per_key_tile
    m, l, acc = _flash(qs, _key_tile(kw_ref), _key_tile(vw_ref), first_tile, last_tile + 1, t_col, slope_col,
                       win_mask)
    o_win = acc / l

    g = jax.nn.sigmoid(gl_ref[0])

    def gate(branch):
        return jnp.concatenate([g[:, branch * NSA_HEADS + h:branch * NSA_HEADS + h + 1] for h in h4], axis=0)

    o = gate(0) * o_cmp + gate(1) * o_slc + gate(2) * o_win
    o_ref[0] = jnp.concatenate([o[h * Q_BLOCK:(h + 1) * Q_BLOCK] for h in h4], axis=1).astype(o_ref.dtype)


def _nsa_attention(proj, kc, vc):
    b, s, _ = proj.shape
    nq = s // Q_BLOCK
    kvb = COL_NSA_KV // HEAD_DIM
    seq = lambda c: pl.BlockSpec((1, s, HEAD_DIM), lambda i, j: (i, 0, c))
    return pl.pallas_call(
        functools.partial(_nsa_body, n_sel=s // SEL_BLOCK),
        grid=(b, nq),
        in_specs=[
            pl.BlockSpec((1, Q_BLOCK, NSA_WIDTH), lambda i, j: (i, j, COL_NSA_Q // NSA_WIDTH)),
            pl.BlockSpec((1, Q_BLOCK, LANES), lambda i, j: (i, j, COL_SMALL // LANES)),
            pl.BlockSpec((1, kc.shape[1], HEAD_DIM), lambda i, j: (i, 0, 0)),
            pl.BlockSpec((1, vc.shape[1], HEAD_DIM), lambda i, j: (i, 0, 0)),
            seq(kvb + 2), seq(kvb + 3), seq(kvb + 4), seq(kvb + 5),
        ],
        out_specs=pl.BlockSpec((1, Q_BLOCK, NSA_WIDTH), lambda i, j: (i, j, 0)),
        out_shape=jax.ShapeDtypeStruct((b, s, NSA_WIDTH), BF16),
        compiler_params=_cparams(("parallel", "arbitrary")),
        name="nsa_attention",
    )(proj, proj, kc, vc, proj, proj, proj, proj)


def _swa_body(sink_ref, q_ref, k_ref, v_ref, o_ref):
    qi = pl.program_id(1)
    rep = SWA_HEADS // SWA_KV_HEADS
    rows = rep * Q_BLOCK
    row = lax.broadcasted_iota(jnp.int32, (rows, 1), 0)
    t_col = qi * Q_BLOCK + _mod(row, Q_BLOCK)
    n_prev = -(-(SWA_WINDOW - 1) // Q_BLOCK)

    def win_mask(kt, dist):
        return (dist >= 0) & (dist < SWA_WINDOW)

    outs = []
    for g in range(SWA_KV_HEADS):
        heads = [g * rep + r for r in range(rep)]
        qs = _stack_heads(q_ref[0], heads).astype(BF16)
        slope_col = _head_const_col(rows, [SWA_SLOPES[h] for h in heads])
        sink_col = _head_const_col(rows, [sink_ref[h] for h in heads])

        q_per_key_tile = KEY_TILE // Q_BLOCK
        m, l, acc = _flash(qs, _key_tile(k_ref, g * HEAD_DIM), _key_tile(v_ref, g * HEAD_DIM),
                           jnp.maximum(qi - n_prev, 0) // q_per_key_tile, qi // q_per_key_tile + 1, t_col,
                           slope_col, win_mask)
        m_all = jnp.maximum(m, sink_col)
        scale = jnp.exp(m - m_all)
        o = acc * scale / (l * scale + jnp.exp(sink_col - m_all))
        outs += [o[r * Q_BLOCK:(r + 1) * Q_BLOCK] for r in range(rep)]
    o_ref[0] = jnp.concatenate(outs, axis=1).astype(o_ref.dtype)


def _swa_attention(proj, sinks):
    b, s, _ = proj.shape
    nq = s // Q_BLOCK
    return pl.pallas_call(
        _swa_body,
        grid=(b, nq),
        in_specs=[
            pl.BlockSpec(memory_space=pltpu.SMEM),
            pl.BlockSpec((1, Q_BLOCK, SWA_WIDTH), lambda i, j: (i, j, COL_SWA_Q // SWA_WIDTH)),
            pl.BlockSpec((1, s, SWA_KV_WIDTH), lambda i, j: (i, 0, COL_SWA_K // SWA_KV_WIDTH)),
            pl.BlockSpec((1, s, SWA_KV_WIDTH), lambda i, j: (i, 0, COL_SWA_V // SWA_KV_WIDTH)),
        ],
        out_specs=pl.BlockSpec((1, Q_BLOCK, SWA_WIDTH), lambda i, j: (i, j, 0)),
        out_shape=jax.ShapeDtypeStruct((b, s, SWA_WIDTH), BF16),
        compiler_params=_cparams(("parallel", "arbitrary")),
        name="swa_attention",
    )(sinks, proj, proj, proj)


GDN_PAIR = 2 * GDN_CHUNK
GDN_STEP_HEADS = 2
GDN_PREP_PAIRS = 4


def _gdn_body(q_ref, k_ref, v_ref, z_ref, ar_ref, br_ref, cwq_ref, cwk_ref, cwv_ref, alog_ref, dtb_ref,
              nw_ref, o_ref, qn_s, kn_s, v_s, u_s, w_s, qd_s, kdt_s, at_s, eg_s, gr_s, br_s):
    s = q_ref.shape[1]
    n_pairs = s // GDN_PAIR
    heads = range(GDN_STEP_HEADS)

    rowi = lax.broadcasted_iota(jnp.int32, (s, 1), 0)

    def conv_silu(x, w):
        y = x * w[GDN_CONV - 1:GDN_CONV, :]
        for j in range(GDN_CONV - 1):
            sh = GDN_CONV - 1 - j
            xs = jnp.where(rowi >= sh, pltpu.roll(x, sh, axis=0), 0.0)
            y = y + xs * w[j:j + 1, :]
        return _silu(y)

    def l2n(t):
        return t * lax.rsqrt(jnp.sum(t * t, axis=-1, keepdims=True) + NORM_EPS)

    for hh in heads:
        cols = slice(hh * HEAD_DIM, (hh + 1) * HEAD_DIM)
        qn_s[hh] = l2n(conv_silu(q_ref[0, :, cols], cwq_ref[:, cols])) * ATTN_SCALE
        kn_s[hh] = l2n(conv_silu(k_ref[0, :, cols], cwk_ref[:, cols]))
        v_s[hh] = conv_silu(v_ref[0, :, cols], cwv_ref[:, cols])
        a_rate = jnp.exp(alog_ref[hh])
        x = ar_ref[0, hh] + dtb_ref[hh]
        gr_s[hh] = -(a_rate * (jnp.maximum(x, 0.0) + jnp.log(1.0 + jnp.exp(-jnp.abs(x)))))
        br_s[hh] = jax.nn.sigmoid(br_ref[0, hh])

    ii = lax.broadcasted_iota(jnp.int32, (GDN_PAIR, GDN_PAIR), 0)
    jj = lax.broadcasted_iota(jnp.int32, (GDN_PAIR, GDN_PAIR), 1)
    same = _div(ii, GDN_CHUNK) == _div(jj, GDN_CHUNK)
    incl = same & (ii >= jj)
    strict = same & (ii > jj)
    diag = ii == jj
    eye = jnp.where(diag, 1.0, 0.0)
    last_of = same & (_mod(jj, GDN_CHUNK) == GDN_CHUNK - 1)
    n_double = int(np.log2(GDN_CHUNK)) - 1

    def tile_rows(p):
        start = p * GDN_PAIR
        return pl.ds(start if isinstance(start, int) else pl.multiple_of(start, GDN_PAIR), GDN_PAIR)

    def prepare_stages(grp):
        chains = [(hh, grp * GDN_PREP_PAIRS + pp) for pp in range(GDN_PREP_PAIRS) for hh in heads]
        rows = [tile_rows(p) for _, p in chains]
        n = range(len(chains))
        beta, gc_col, gl_col, decay = [], [], [], []
        for hh, p in chains:
            g_row = gr_s[hh, pl.ds(p, 1), :]
            beta.append(jnp.sum(jnp.where(diag, br_s[hh, pl.ds(p, 1), :], 0.0), axis=1, keepdims=True))
            gcc = jnp.sum(jnp.where(incl, g_row, 0.0), axis=1, keepdims=True)
            gcr = jnp.sum(jnp.where(diag, gcc, 0.0), axis=0, keepdims=True)
            gc_col.append(gcc)
            gl_col.append(jnp.sum(jnp.where(last_of, gcr, 0.0), axis=1, keepdims=True))
            decay.append(jnp.exp(jnp.where(incl, gcc - gcr, NEG_INF)))
        q = [qn_s[hh, rows[c], :] for c, (hh, _) in enumerate(chains)]
        k = [kn_s[hh, rows[c], :] for c, (hh, _) in enumerate(chains)]
        v = [v_s[hh, rows[c], :] for c, (hh, _) in enumerate(chains)]
        kb = [k[c] * beta[c] for c in n]
        kf = [k[c].astype(BF16) for c in n]
        kk = [_dot_nt(kb[c].astype(BF16), kf[c]) for c in n]
        qk = [_dot_nt(q[c].astype(BF16), kf[c]) for c in n]
        yield
        pw = [-jnp.where(strict, kk[c] * decay[c], 0.0) for c in n]
        tinv = [eye + pw[c] for c in n]
        for _ in range(n_double):
            pw = [_dot_hi(pw[c], pw[c]) for c in n]
            yield
            tinv = [tinv[c] + _dot_hi(tinv[c], pw[c]) for c in n]
            yield
        egc = [jnp.exp(gc_col[c]) for c in n]
        sol = [_dot_hi(tinv[c], jnp.concatenate([v[c] * beta[c], kb[c] * egc[c]], axis=1)) for c in n]
        yield
        for c, (hh, _) in enumerate(chains):
            u_s[hh, rows[c], :] = sol[c][:, :HEAD_DIM]
            w_s[hh, rows[c], :] = sol[c][:, HEAD_DIM:]
            at_s[hh, rows[c], :] = jnp.where(incl, qk[c] * decay[c], 0.0)
            qd_s[hh, rows[c], :] = q[c] * egc[c]
            kdt_s[hh, rows[c], :] = (k[c] * jnp.exp(gl_col[c] - gc_col[c])).T
            eg_s[hh, rows[c], :] = jnp.broadcast_to(jnp.exp(gl_col[c]), (GDN_PAIR, HEAD_DIM))

    rin = lax.broadcasted_iota(jnp.int32, (GDN_PAIR, 1), 0)
    nw = nw_ref[...]

    def scan_stages(grp, states):
        for pp in range(GDN_PREP_PAIRS):
            rows = tile_rows(grp * GDN_PREP_PAIRS + pp)
            u = [u_s[hh, rows, :] for hh in heads]
            w = [w_s[hh, rows, :].astype(BF16) for hh in heads]
            qd = [qd_s[hh, rows, :].astype(BF16) for hh in heads]
            at = [at_s[hh, rows, :].astype(BF16) for hh in heads]
            kdt = [kdt_s[hh, rows, :].astype(BF16) for hh in heads]
            eg = [eg_s[hh, rows, :] for hh in heads]
            o = [jnp.zeros((GDN_PAIR, HEAD_DIM), F32) for _ in heads]
            for c in range(GDN_PAIR // GDN_CHUNK):
                in_c = _div(rin, GDN_CHUNK) == c
                sb = [states[hh].astype(BF16) for hh in heads]
                ws = [_dot(w[hh], sb[hh]) for hh in heads]
                qs = [_dot(qd[hh], sb[hh]) for hh in heads]
                yield
                vb = [jnp.where(in_c, u[hh] - ws[hh], 0.0).astype(BF16) for hh in heads]
                av = [_dot(at[hh], vb[hh]) for hh in heads]
                kv = [_dot(kdt[hh], vb[hh]) for hh in heads]
                yield
                o = [jnp.where(in_c, qs[hh] + av[hh], o[hh]) for hh in heads]
                for hh in heads:
                    states[hh] = states[hh] * eg[hh][c * GDN_CHUNK:c * GDN_CHUNK + 1, :] + kv[hh]
            for hh in heads:
                cols = slice(hh * HEAD_DIM, (hh + 1) * HEAD_DIM)
                on = o[hh] * lax.rsqrt(jnp.mean(o[hh] * o[hh], axis=-1, keepdims=True) + NORM_EPS) * nw
                o_ref[0, rows, cols] = (on * _silu(z_ref[0, rows, cols])).astype(o_ref.dtype)

    def alternate(*phases):
        live = list(phases)
        while live:
            for ph in list(live):
                if next(ph, live) is live:
                    live.remove(ph)

    n_groups = n_pairs // GDN_PREP_PAIRS
    alternate(prepare_stages(0))

    def group_step(grp, states):
        states = list(states)
        alternate(prepare_stages(grp), scan_stages(grp - 1, states))
        return tuple(states)

    states = lax.fori_loop(1, n_groups, group_step, tuple(jnp.zeros((HEAD_DIM, HEAD_DIM), F32) for _ in heads))
    alternate(scan_stages(n_groups - 1, list(states)))


def _gdn_mixer(proj, conv_w, a_log, dt_bias, norm_w):
    b, s, _ = proj.shape
    h = GDN_HEADS
    nh = GDN_STEP_HEADS
    n_pairs = s // GDN_PAIR
    wid = nh * HEAD_DIM

    def rows_of(col):
        return proj[:, :, col:col + h].transpose(0, 2, 1).reshape(b, h, n_pairs, GDN_PAIR)

    seq = lambda off: pl.BlockSpec((1, s, wid), lambda i, j: (i, 0, (COL_GDN + off * GDN_WIDTH) // wid + j))
    gate = pl.BlockSpec((1, nh, n_pairs, GDN_PAIR), lambda i, j: (i, j, 0, 0))
    cw = lambda off: pl.BlockSpec((GDN_CONV, wid), lambda i, j: (0, off * (h // nh) + j))
    scalar = pl.BlockSpec((nh, 1, 1), lambda i, j: (j, 0, 0))
    seq_f32 = pltpu.VMEM((nh, s, HEAD_DIM), F32)
    row_f32 = pltpu.VMEM((nh, n_pairs, GDN_PAIR), F32)
    return pl.pallas_call(
        _gdn_body,
        grid=(b, h // nh),
        in_specs=[
            seq(0), seq(1), seq(2), seq(3),
            gate, gate,
            cw(0), cw(1), cw(2),
            scalar, scalar,
            pl.BlockSpec((1, HEAD_DIM), lambda i, j: (0, 0)),
        ],
        out_specs=pl.BlockSpec((1, s, wid), lambda i, j: (i, 0, j)),
        out_shape=jax.ShapeDtypeStruct((b, s, GDN_WIDTH), BF16),
        scratch_shapes=[seq_f32] * 9 + [row_f32, row_f32],
        compiler_params=_cparams(("parallel", "arbitrary")),
        name="gdn_mixer",
    )(proj, proj, proj, proj, rows_of(COL_SMALL + SMALL_A), rows_of(COL_SMALL + SMALL_B), conv_w, conv_w, conv_w,
      a_log.reshape(h, 1, 1), dt_bias.reshape(h, 1, 1), norm_w.reshape(1, HEAD_DIM))


def _outproj_body(on_ref, og_ref, os_ref, x_ref, w_ref, o_ref):
    mix = jnp.concatenate([on_ref[...], og_ref[...], os_ref[...]], axis=1)
    o_ref[...] = x_ref[...] + _dot(mix, w_ref[...])


def _outproj_router_body(on_ref, og_ref, os_ref, x_ref, w_ref, nw_ref, r_ref, o_ref, h_ref, lg_ref):
    mix = jnp.concatenate([on_ref[...], og_ref[...], os_ref[...]], axis=1)
    x1 = x_ref[...] + _dot(mix, w_ref[...])
    o_ref[...] = x1
    h = _rms(x1, nw_ref[...])
    h_ref[...] = h
    lg_ref[...] = _dot_hi(h, r_ref[...])


def _out_proj(o_nsa, o_gdn, o_swa, x2d, w_bf16, ffn_norm=None, router=None, tm=512):
    n, d = x2d.shape
    row = lambda wid: pl.BlockSpec((tm, wid), lambda i: (i, 0))
    full = lambda a, c: pl.BlockSpec((a, c), lambda i: (0, 0))
    in_specs = [row(NSA_WIDTH), row(GDN_WIDTH), row(SWA_WIDTH), row(d), full(d, d)]
    args = [o_nsa, o_gdn, o_swa, x2d, w_bf16]
    if router is None:
        return pl.pallas_call(
            _outproj_body, grid=(n // tm,), in_specs=in_specs, out_specs=row(d),
            out_shape=jax.ShapeDtypeStruct((n, d), F32),
            compiler_params=_cparams(("parallel",)), name="out_proj",
        )(*args)
    router_pad = jnp.pad(router, ((0, 0), (0, LANES - router.shape[1])))
    return pl.pallas_call(
        _outproj_router_body, grid=(n // tm,),
        in_specs=in_specs + [full(1, d), full(d, LANES)],
        out_specs=[row(d), row(d), row(LANES)],
        out_shape=[jax.ShapeDtypeStruct((n, d), F32), jax.ShapeDtypeStruct((n, d), F32),
                   jax.ShapeDtypeStruct((n, LANES), F32)],
        compiler_params=_cparams(("parallel",)), name="out_proj_router",
    )(*args, ffn_norm.reshape(1, d), router_pad)


def _swiglu_tile(h, wg, wu, wd):
    a = _dot(h, wg.astype(BF16))
    b = _dot(h, wu.astype(BF16))
    return _dot((_silu(a) * b).astype(BF16), wd.astype(BF16))


def _dense_body(x_ref, nw_ref, wg_ref, wu_ref, wd_ref, o_ref, h_ref):
    @pl.when(pl.program_id(1) == 0)
    def _():
        x = x_ref[...]
        h_ref[...] = _rms(x, nw_ref[...]).astype(BF16)
        o_ref[...] = x

    o_ref[...] += _swiglu_tile(h_ref[...], wg_ref[...], wu_ref[...], wd_ref[...])


def _dense_ffn(x2d, norm_w, wg, wu, wd, tm=1024, tf=256):
    n, d = x2d.shape
    ff = wg.shape[1]
    return pl.pallas_call(
        _dense_body,
        grid=(n // tm, ff // tf),
        in_specs=[
            pl.BlockSpec((tm, d), lambda i, f: (i, 0)),
            pl.BlockSpec((1, d), lambda i, f: (0, 0)),
            pl.BlockSpec((d, tf), lambda i, f: (0, f)),
            pl.BlockSpec((d, tf), lambda i, f: (0, f)),
            pl.BlockSpec((tf, d), lambda i, f: (f, 0)),
        ],
        out_specs=pl.BlockSpec((tm, d), lambda i, f: (i, 0)),
        out_shape=jax.ShapeDtypeStruct((n, d), F32),
        scratch_shapes=[pltpu.VMEM((tm, d), BF16)],
        compiler_params=_cparams(("parallel", "arbitrary")),
        name="dense_ffn",
    )(x2d, norm_w.reshape(1, d), wg, wu, wd)


def _row_gather(idx_ref, base, count, src_hbm, dst, sem):
    def body(r, carry):
        t = idx_ref[base + r]
        pltpu.make_async_copy(src_hbm.at[pl.ds(t, 1), :], dst.at[pl.ds(r, 1), :], sem).start()
        return carry

    lax.fori_loop(0, count, body, 0, unroll=8)


def _row_gather_wait(count, src_hbm, dst, sem):
    pltpu.make_async_copy(src_hbm.at[pl.ds(0, count), :], dst, sem).wait()


MOE_STEP_ROWS = 32


def _moe_body(be_ref, nu_ref, br_ref, tok_ref, h_hbm, wg_ref, wu_ref, wd_ref, o_ref, xbuf, x_s, sem, *, nf):
    i = pl.program_id(0)
    f = pl.program_id(1)
    tm = x_s.shape[0]
    n_used = nu_ref[0]
    used = i < n_used
    more = i + 1 < n_used
    small = br_ref[i] <= tm // 2
    head_rows = tm - MOE_STEP_ROWS * nf
    assert head_rows >= 0

    @pl.when(f == 0)
    def _():
        o_ref[...] = jnp.zeros_like(o_ref)

        @pl.when(i == 0)
        def _():
            _row_gather(tok_ref, 0, tm, h_hbm, xbuf, sem)

        @pl.when(used)
        def _():
            _row_gather_wait(tm, h_hbm, xbuf, sem)
            x_s[...] = xbuf[...].astype(BF16)

        @pl.when(more)
        def _():
            _row_gather(tok_ref, (i + 1) * tm, head_rows, h_hbm, xbuf, sem)

    def accumulate(rows):
        o_ref[0:rows, :] += _swiglu_tile(x_s[0:rows, :], wg_ref[0], wu_ref[0], wd_ref[0])

    def fetch_next_rows():
        base = head_rows + f * MOE_STEP_ROWS
        for u in range(MOE_STEP_ROWS):
            t = tok_ref[(i + 1) * tm + base + u]
            pltpu.make_async_copy(h_hbm.at[pl.ds(t, 1), :], xbuf.at[pl.ds(base + u, 1), :], sem).start()

    for rows, fits in ((tm // 2, small), (tm, jnp.logical_not(small))):
        @pl.when(more & fits)
        def _(rows=rows):
            fetch_next_rows()
            accumulate(rows)

        @pl.when(used & jnp.logical_not(more) & fits)
        def _(rows=rows):
            accumulate(rows)


def _moe_ffn(h, row_tok, blk_expert, n_used, blk_rows, wg, wu, wd, tm, tf=256):
    p = row_tok.shape[0]
    d = h.shape[1]
    ff = wg.shape[2]
    nb = p // tm
    nf = ff // tf

    def fsel(i, f, nu):
        return jnp.where(i < nu[0], f, nf - 1)

    grid_spec = pltpu.PrefetchScalarGridSpec(
        num_scalar_prefetch=4,
        grid=(nb, nf),
        in_specs=[
            pl.BlockSpec(memory_space=pl.ANY),
            pl.BlockSpec((1, d, tf), lambda i, f, be, nu, br, tok: (be[i], 0, fsel(i, f, nu))),
            pl.BlockSpec((1, d, tf), lambda i, f, be, nu, br, tok: (be[i], 0, fsel(i, f, nu))),
            pl.BlockSpec((1, tf, d), lambda i, f, be, nu, br, tok: (be[i], fsel(i, f, nu), 0)),
        ],
        out_specs=pl.BlockSpec((tm, d), lambda i, f, be, nu, br, tok: (i, 0)),
        scratch_shapes=[pltpu.VMEM((tm, d), F32), pltpu.VMEM((tm, d), BF16), pltpu.SemaphoreType.DMA(())],
    )
    return pl.pallas_call(
        functools.partial(_moe_body, nf=nf),
        grid_spec=grid_spec,
        out_shape=jax.ShapeDtypeStruct((p, d), F32),
        compiler_params=_cparams(("arbitrary", "arbitrary")),
        name="moe_ffn",
    )(blk_expert, n_used, blk_rows, row_tok, h, wg, wu, wd)


def _moe_layer(h, logits, wg, wu, wd, tm=1024):
    n = h.shape[0]
    top_logit, top_idx = lax.top_k(logits, TOP_K)
    gate = jax.nn.softmax(top_logit, axis=-1)
    a = n * TOP_K
    e_flat = top_idx.reshape(a)
    onehot = (e_flat[:, None] == jnp.arange(N_EXPERTS, dtype=e_flat.dtype)[None]).astype(jnp.int32)
    rank = jnp.sum((jnp.cumsum(onehot, axis=0) - onehot) * onehot, axis=1)
    counts = jnp.sum(onehot, axis=0)
    padded = (counts + tm - 1) // tm * tm
    pad_end = jnp.cumsum(padded)
    pad_start = pad_end - padded
    dest = (pad_start[e_flat] + rank).astype(jnp.int32)
    nb = a // tm + N_EXPERTS
    n_used = (pad_end[-1] // tm).astype(jnp.int32)
    blk = jnp.minimum(jnp.arange(nb, dtype=jnp.int32), n_used - 1)
    blk_expert = jnp.sum((blk[:, None] * tm >= pad_end[None, :]).astype(jnp.int32), axis=1)
    blk_expert = jnp.minimum(blk_expert, N_EXPERTS - 1)
    blk_rows = jnp.clip(counts[blk_expert] - (blk * tm - pad_start[blk_expert]), 0, tm).astype(jnp.int32)
    row_tok = jnp.zeros((nb * tm,), jnp.int32).at[dest].set(jnp.arange(a, dtype=jnp.int32) // TOP_K)
    ys = _moe_ffn(h, row_tok, blk_expert, n_used.reshape(1), blk_rows, wg, wu, wd, tm)
    return ys, dest, gate


def _combine_body(d_ref, x_ref, g_ref, w_ref, ys_hbm, o_ref, buf, sem, *, final_norm):
    j = pl.program_id(0)
    tt = x_ref.shape[0]
    slot = j % 2

    def start(step, sl):
        for k in range(TOP_K):
            _row_gather(d_ref, (step * TOP_K + k) * tt, tt, ys_hbm, buf.at[sl, k], sem.at[sl])

    @pl.when(j == 0)
    def _():
        start(0, 0)

    for k in range(TOP_K):
        _row_gather_wait(tt, ys_hbm, buf.at[slot, k], sem.at[slot])

    @pl.when(j + 1 < pl.num_programs(0))
    def _():
        start(j + 1, 1 - slot)

    g = g_ref[...]
    f = buf[slot, 0] * g[:, 0:1]
    for k in range(1, TOP_K):
        f = f + buf[slot, k] * g[:, k:k + 1]
    y = x_ref[...] + f
    o_ref[...] = _rms(y, w_ref[...]) if final_norm else y


def _moe_combine(x2d, ys, dest, gate, norm_w, final_norm, tt=512):
    n, d = x2d.shape
    steps = n // tt
    order = dest.reshape(steps, tt, TOP_K).transpose(0, 2, 1).reshape(n * TOP_K)
    grid_spec = pltpu.PrefetchScalarGridSpec(
        num_scalar_prefetch=1,
        grid=(steps,),
        in_specs=[
            pl.BlockSpec((tt, d), lambda j, dr: (j, 0)),
            pl.BlockSpec((tt, TOP_K), lambda j, dr: (j, 0)),
            pl.BlockSpec((1, d), lambda j, dr: (0, 0)),
            pl.BlockSpec(memory_space=pl.ANY),
        ],
        out_specs=pl.BlockSpec((tt, d), lambda j, dr: (j, 0)),
        scratch_shapes=[pltpu.VMEM((2, TOP_K, tt, d), F32), pltpu.SemaphoreType.DMA((2,))],
    )
    return pl.pallas_call(
        functools.partial(_combine_body, final_norm=final_norm),
        grid_spec=grid_spec,
        out_shape=jax.ShapeDtypeStruct((n, d), F32),
        compiler_params=_cparams(("arbitrary",)),
        name="moe_combine",
    )(order, x2d, gate, norm_w.reshape(1, d), ys)


def kernel(x, attn_norm, w_in, cmp_pos_k, cmp_pos_v, cmp_w1_k, cmp_w2_k, cmp_w1_v, cmp_w2_v, gdn_conv_w, gdn_a_log, gdn_dt_bias, gdn_norm_w, swa_sinks, w_out, ffn_norm, dense_w_gate, dense_w_up, dense_w_down, moe_router, moe_w_gate, moe_w_up, moe_w_down, final_norm):
    b, s, d = x.shape
    n = b * s
    depth = w_in.shape[0]
    assert depth % 2 == 0, "the trunk ends on an expert layer, whose combine feeds the final norm"
    x2d = x.reshape(n, d)
    w_in_cols = _reorder_w_in(w_in)
    for layer in range(depth):
        proj = _in_proj(x2d, attn_norm[layer], w_in_cols, layer).reshape(b, s, PROJ_COLS)
        n_strides = s // CMP_STRIDE
        ck = proj[:, :, COL_NSA_KV:COL_NSA_KV + HEAD_DIM].reshape(b, n_strides, CMP_STRIDE * HEAD_DIM)
        cv = proj[:, :, COL_NSA_KV + HEAD_DIM:COL_NSA_KV + 2 * HEAD_DIM].reshape(b, n_strides, CMP_STRIDE * HEAD_DIM)
        kc, vc = _nsa_compress(ck, cv, cmp_pos_k[layer], cmp_pos_v[layer], cmp_w1_k[layer], cmp_w2_k[layer],
                               cmp_w1_v[layer], cmp_w2_v[layer])
        o_nsa = _nsa_attention(proj, kc, vc).reshape(n, NSA_WIDTH)
        o_gdn = _gdn_mixer(proj, gdn_conv_w[layer], gdn_a_log[layer], gdn_dt_bias[layer],
                           gdn_norm_w[layer]).reshape(n, GDN_WIDTH)
        o_swa = _swa_attention(proj, swa_sinks[layer]).reshape(n, SWA_WIDTH)
        w_o = w_out[layer].astype(BF16)
        i = layer // 2
        if layer % 2 == 0:
            x1 = _out_proj(o_nsa, o_gdn, o_swa, x2d, w_o)
            x2d = _dense_ffn(x1, ffn_norm[layer], dense_w_gate[i], dense_w_up[i], dense_w_down[i])
        else:
            x1, h2, logits = _out_proj(o_nsa, o_gdn, o_swa, x2d, w_o, ffn_norm[layer], moe_router[i])
            ys, dest, gate = _moe_layer(h2, logits[:, :N_EXPERTS], moe_w_gate[i], moe_w_up[i], moe_w_down[i])
            x2d = _moe_combine(x1, ys, dest, gate, final_norm, final_norm=layer == depth - 1)
    return x2d.reshape(b, s, d)
```
